```python
import math
import jax, jax.numpy as jnp
from jax import lax
import numpy as np

D_MODEL = 4096
BATCH = 2
SEQ = 8192
DEPTH = 1

HEAD_DIM = 128
N_DIFF_HEADS = D_MODEL // 512
ATTN_QK = N_DIFF_HEADS * 2 * HEAD_DIM
ATTN_V = N_DIFF_HEADS * 2 * HEAD_DIM
ROT_DIM = HEAD_DIM // 4
ROPE_THETA = 500000.0
Q_BLOCK = 128
FOURIER_WIDTH = D_MODEL // 2
FOURIER_GROUP = 256
N_FOURIER_GROUPS = FOURIER_WIDTH // FOURIER_GROUP
IN_COLS = 2 * ATTN_QK + ATTN_V + FOURIER_WIDTH
D_FF = ((8 * D_MODEL // 3 + 255) // 256) * 256
CONV_WIDTH = 3
ALPHA = (2.0 * DEPTH) ** 0.25
BETA = (8.0 * DEPTH) ** -0.25
LN_EPS = 1e-5

kernel_name = 'hybrid_diffattn_fnet_convffn_deepnorm'


def layer_norm(x, g, b):
    xf = x.astype(jnp.float32)
    mu = jnp.mean(xf, axis=-1, keepdims=True)
    var = jnp.mean(jnp.square(xf - mu), axis=-1, keepdims=True)
    return ((xf - mu) * lax.rsqrt(var + LN_EPS) * g.astype(jnp.float32) + b.astype(jnp.float32)).astype(x.dtype)


def rms_norm(x, g):
    xf = x.astype(jnp.float32)
    ms = jnp.mean(jnp.square(xf), axis=-1, keepdims=True)
    return (xf * lax.rsqrt(ms + LN_EPS) * g.astype(jnp.float32)).astype(x.dtype)


def partial_rotary(t):
    s = t.shape[1]
    inv_freq = ROPE_THETA ** (-jnp.arange(0, ROT_DIM, 2, dtype=jnp.float32) / ROT_DIM)
    ang = jnp.arange(s, dtype=jnp.float32)[:, None] * inv_freq[None, :]
    cos = jnp.cos(ang).astype(t.dtype)[None, :, None, None, :]
    sin = jnp.sin(ang).astype(t.dtype)[None, :, None, None, :]
    half = ROT_DIM // 2
    x1 = t[..., :half]
    x2 = t[..., half:ROT_DIM]
    return jnp.concatenate([x1 * cos - x2 * sin, x2 * cos + x1 * sin, t[..., ROT_DIM:]], axis=-1)


def diff_attention(u_q, u_k, u_v, lq1, lk1, lq2, lk2, subln_g, lambda_init):
    b, s, _ = u_q.shape
    q = partial_rotary(u_q.reshape(b, s, N_DIFF_HEADS, 2, HEAD_DIM))
    k = partial_rotary(u_k.reshape(b, s, N_DIFF_HEADS, 2, HEAD_DIM))
    q = q.transpose(0, 2, 3, 1, 4) * (HEAD_DIM ** -0.5)
    k = k.transpose(0, 2, 3, 1, 4)
    v = u_v.reshape(b, s, N_DIFF_HEADS, 2 * HEAD_DIM).transpose(0, 2, 1, 3)
    lam = (jnp.exp(jnp.sum(lq1.astype(jnp.float32) * lk1.astype(jnp.float32)))
           - jnp.exp(jnp.sum(lq2.astype(jnp.float32) * lk2.astype(jnp.float32)))
           + lambda_init)

    def query_block(i):
        qb = lax.dynamic_slice_in_dim(q, i * Q_BLOCK, Q_BLOCK, axis=3)
        sc = jnp.einsum('bhcqd,bhckd->bhcqk', qb, k, preferred_element_type=jnp.float32)
        p = jax.nn.softmax(sc, axis=-1)
        a = (p[:, :, 0] - lam * p[:, :, 1]).astype(v.dtype)
        return jnp.einsum('bhqk,bhkv->bhqv', a, v)

    o = lax.map(query_block, jnp.arange(s // Q_BLOCK))
    o = o.transpose(1, 0, 3, 2, 4).reshape(b, s, N_DIFF_HEADS, 2 * HEAD_DIM)
    o = rms_norm(o, subln_g) * (1.0 - lambda_init)
    return o.reshape(b, s, ATTN_V)


def fourier_mix(u_f):
    b, s, _ = u_f.shape
    ug = u_f.astype(jnp.float32).reshape(b, s, N_FOURIER_GROUPS, FOURIER_GROUP)
    y = jnp.fft.fft2(ug, axes=(1, 3), norm='ortho').real
    return y.reshape(b, s, FOURIER_WIDTH).astype(u_f.dtype)


def hybrid_mixer(h, w_in, lq1, lk1, lq2, lk2, subln_g, w_attn_o, w_fourier, w_gate, b_gate, w_mix_out, lambda_init):
    u = h @ w_in
    u_q, u_k, u_v, u_f = jnp.split(u, [ATTN_QK, 2 * ATTN_QK, 2 * ATTN_QK + ATTN_V], axis=-1)
    y_attn = diff_attention(u_q, u_k, u_v, lq1, lk1, lq2, lk2, subln_g, lambda_init) @ w_attn_o
    y_four = fourier_mix(u_f) @ w_fourier
    g = jax.nn.sigmoid(h @ w_gate + b_gate)
    g_attn, g_four = jnp.split(g, 2, axis=-1)
    return (g_attn * y_attn + g_four * y_four) @ w_mix_out


def conv_ffn(h, w_up, conv_w, conv_b, w_down):
    a = h @ w_up
    ap = jnp.pad(a, ((0, 0), (1, 1), (0, 0)))
    c = ap[:, :-2] * conv_w[0] + ap[:, 1:-1] * conv_w[1] + ap[:, 2:] * conv_w[2] + conv_b
    gate, val = jnp.split(c, 2, axis=-1)
    return (jax.nn.silu(gate) * val) @ w_down


def setup_inputs(seed: int = 0) -> dict:
    key = jax.random.key(seed)
    ks = iter(jax.random.split(key, 32))
    L, D = DEPTH, D_MODEL

    def nrm(shape, scale):
        return jax.random.normal(next(ks), shape, jnp.float32) * scale

    x = nrm((BATCH, SEQ, D), 1.0)
    ln_emb_g = 1.0 + nrm((D,), 0.01)
    ln_emb_b = nrm((D,), 0.01)
    w_in = jnp.concatenate([
        nrm((L, D, ATTN_QK), D ** -0.5),
        nrm((L, D, ATTN_QK), D ** -0.5),
        nrm((L, D, ATTN_V), D ** -0.5 * BETA),
        nrm((L, D, FOURIER_WIDTH), D ** -0.5),
    ], axis=-1)
    lambda_q1 = nrm((L, HEAD_DIM), 0.1)
    lambda_k1 = nrm((L, HEAD_DIM), 0.1)
    lambda_q2 = nrm((L, HEAD_DIM), 0.1)
    lambda_k2 = nrm((L, HEAD_DIM), 0.1)
    subln_g = 1.0 + nrm((L, 2 * HEAD_DIM), 0.01)
    w_attn_o = nrm((L, ATTN_V, D), ATTN_V ** -0.5 * BETA)
    w_fourier = nrm((L, FOURIER_WIDTH, D), FOURIER_WIDTH ** -0.5 * BETA)
    w_gate = nrm((L, D, 2 * D), D ** -0.5)
    b_gate = nrm((L, 2 * D), 0.01)
    w_mix_out = nrm((L, D, D), D ** -0.5 * BETA)
    ln1_g = 1.0 + nrm((L, D), 0.01)
    ln1_b = nrm((L, D), 0.01)
    w_up = nrm((L, D, 2 * D_FF), D ** -0.5 * BETA)
    conv_w = nrm((L, CONV_WIDTH, 2 * D_FF), CONV_WIDTH ** -0.5)
    conv_b = nrm((L, 2 * D_FF), 0.01)
    w_down = nrm((L, D_FF, D), D_FF ** -0.5 * BETA)
    ln2_g = 1.0 + nrm((L, D), 0.01)
    ln2_b = nrm((L, D), 0.01)
    return {'x': x, 'ln_emb_g': ln_emb_g, 'ln_emb_b': ln_emb_b, 'w_in': w_in,
            'lambda_q1': lambda_q1, 'lambda_k1': lambda_k1, 'lambda_q2': lambda_q2, 'lambda_k2': lambda_k2,
            'subln_g': subln_g, 'w_attn_o': w_attn_o, 'w_fourier': w_fourier,
            'w_gate': w_gate, 'b_gate': b_gate, 'w_mix_out': w_mix_out,
            'ln1_g': ln1_g, 'ln1_b': ln1_b, 'w_up': w_up, 'conv_w': conv_w, 'conv_b': conv_b,
            'w_down': w_down, 'ln2_g': ln2_g, 'ln2_b': ln2_b}


def reference(x, ln_emb_g, ln_emb_b, w_in, lambda_q1, lambda_k1, lambda_q2, lambda_k2, subln_g,
              w_attn_o, w_fourier, w_gate, b_gate, w_mix_out, ln1_g, ln1_b,
              w_up, conv_w, conv_b, w_down, ln2_g, ln2_b):
    h = layer_norm(x, ln_emb_g, ln_emb_b)
    for l in range(DEPTH):
        lambda_init = 0.8 - 0.6 * math.exp(-0.3 * l)
        m = hybrid_mixer(h, w_in[l], lambda_q1[l], lambda_k1[l], lambda_q2[l], lambda_k2[l], subln_g[l],
                         w_attn_o[l], w_fourier[l], w_gate[l], b_gate[l], w_mix_out[l], lambda_init)
        h = layer_norm(ALPHA * h + m, ln1_g[l], ln1_b[l])
        f = conv_ffn(h, w_up[l], conv_w[l], conv_b[l], w_down[l])
        h = layer_norm(ALPHA * h + f, ln2_g[l], ln2_b[l])
    return h
```

```python
import functools
import math

import numpy as np
import jax
import jax.numpy as jnp
from jax import lax
from jax.experimental import pallas as pl
from jax.experimental.pallas import tpu as pltpu

F32 = jnp.float32
BF16 = jnp.bfloat16

HEAD_DIM = 128
ROT_DIM = HEAD_DIM // 4
ROPE_THETA = 500000.0
FOURIER_GROUP = 256
CONV_WIDTH = 3
DEPTH = 1
ALPHA = (2.0 * DEPTH) ** 0.25
LN_EPS = 1e-5
LAMBDA_INIT = 0.8 - 0.6 * math.exp(-0.3 * 0)

LANES = 128
BF16_SUBLANES = 16
V7X_VMEM_BYTES = 64 * 1024 * 1024
VMEM_CAP_BYTES = 60000 * 1024
SEQ_DFT_MAJOR = 128
HALO = BF16_SUBLANES

TILES = dict(ln_rows=256, in_bm=1024, in_bn=1024, attn_tq=1024, attn_tk=1024, chan_bm=2048, dft_a_jb=4,
             dft_b_cols=16384, gate_bm=512, gate_bn=512, mix_bm=1024, mix_bn=512, up_bm=1024, up_bn=256,
             down_bm=512, down_bn=256)


def _vmem_limit(pipelined_bytes, resident_bytes=0):
    need = 2 * pipelined_bytes + resident_bytes + (2 << 20)
    return int(min(max(need, 16 << 20), VMEM_CAP_BYTES))


def _blk(n, pref, align):
    if n <= pref:
        return n
    b = (pref // align) * align
    while b >= align:
        if n % b == 0:
            return b
        b -= align
    raise ValueError(f"no block of alignment {align} divides {n}")


def _params(sem, vmem):
    return pltpu.CompilerParams(dimension_semantics=sem, vmem_limit_bytes=vmem)


def _ln_math(x, g, b):
    mu = jnp.mean(x, axis=-1, keepdims=True)
    xc = x - mu
    var = jnp.mean(xc * xc, axis=-1, keepdims=True)
    return xc * lax.rsqrt(var + LN_EPS) * g + b


def _ln_body(x_ref, g_ref, b_ref, *o_refs):
    y = _ln_math(x_ref[...].astype(F32), g_ref[...], b_ref[...])
    for o_ref in o_refs:
        o_ref[...] = y.astype(o_ref.dtype)


def _layer_norm(x, g, b, out_dtypes):
    m, d = x.shape
    bm = _blk(m, TILES["ln_rows"], BF16_SUBLANES)
    row = pl.BlockSpec((bm, d), lambda i: (i, 0))
    vec = pl.BlockSpec((1, d), lambda i: (0, 0))
    blk_bytes = bm * d * (4 + sum(jnp.dtype(t).itemsize for t in out_dtypes))
    return pl.pallas_call(
        _ln_body,
        grid=(m // bm,),
        in_specs=[row, vec, vec],
        out_specs=[row] * len(out_dtypes),
        out_shape=[jax.ShapeDtypeStruct((m, d), t) for t in out_dtypes],
        compiler_params=_params(("parallel",), _vmem_limit(blk_bytes, 4 * bm * d * 4)),
        name="layer_norm",
    )(x, g.reshape(1, d).astype(F32), b.reshape(1, d).astype(F32))


def _in_proj_body(x_ref, w_ref, cos_ref, sin_ref, o_ref, *, n_q_blocks, n_rot_blocks, q_scale):
    j = pl.program_id(1)
    acc = jnp.dot(x_ref[...], w_ref[...], preferred_element_type=F32)
    bm, bn = acc.shape

    @pl.when(j < n_rot_blocks)
    def _():
        c = cos_ref[...]
        s = sin_ref[...]
        scale = jnp.where(j < n_q_blocks, q_scale, 1.0).astype(F32)
        lane = lax.broadcasted_iota(jnp.int32, (bm, LANES), 1)
        first_half = lane < (ROT_DIM // 2)
        for cb in range(bn // LANES):
            t = acc[:, cb * LANES:(cb + 1) * LANES]
            partner = jnp.where(first_half,
                                pltpu.roll(t, LANES - ROT_DIM // 2, 1),
                                pltpu.roll(t, ROT_DIM // 2, 1))
            o_ref[:, cb * LANES:(cb + 1) * LANES] = ((t * c + partner * s) * scale).astype(o_ref.dtype)

    @pl.when(j >= n_rot_blocks)
    def _():
        o_ref[...] = acc.astype(o_ref.dtype)


def _rotary_tables(seq):
    inv_freq = ROPE_THETA ** (-jnp.arange(0, ROT_DIM, 2, dtype=F32) / ROT_DIM)
    ang = jnp.arange(seq, dtype=F32)[:, None] * inv_freq[None, :]
    cos, sin = jnp.cos(ang), jnp.sin(ang)
    pad = HEAD_DIM - ROT_DIM
    cos_t = jnp.concatenate([cos, cos, jnp.ones((seq, pad), F32)], axis=1)
    sin_t = jnp.concatenate([-sin, sin, jnp.zeros((seq, pad), F32)], axis=1)
    return cos_t, sin_t


def _in_proj(h0b, w_in_b, seq, qk_cols):
    m, d = h0b.shape
    n = w_in_b.shape[1]
    bm = _blk(seq, TILES["in_bm"], BF16_SUBLANES)
    bn = _blk(qk_cols, TILES["in_bn"], LANES)
    cos_t, sin_t = _rotary_tables(seq)
    tab = pl.BlockSpec((bm, LANES), lambda i, j: (i % (seq // bm), 0))
    body = functools.partial(_in_proj_body, n_q_blocks=qk_cols // bn, n_rot_blocks=2 * qk_cols // bn,
                             q_scale=HEAD_DIM ** -0.5)
    blk_bytes = bm * d * 2 + d * bn * 2 + bm * bn * 2 + 2 * bm * LANES * 4
    return pl.pallas_call(
        body,
        grid=(m // bm, n // bn),
        in_specs=[pl.BlockSpec((bm, d), lambda i, j: (i, 0)),
                  pl.BlockSpec((d, bn), lambda i, j: (0, j)),
                  tab, tab],
        out_specs=pl.BlockSpec((bm, bn), lambda i, j: (i, j)),
        out_shape=jax.ShapeDtypeStruct((m, n), BF16),
        compiler_params=_params(("parallel", "arbitrary"), _vmem_limit(blk_bytes, 2 * bm * bn * 4)),
        name="in_proj",
    )(h0b, w_in_b, cos_t, sin_t)


def _attn_body(q_ref, k_ref, v_ref, lq1_ref, lk1_ref, lq2_ref, lk2_ref, g_ref, o_ref,
               m_ref, l_ref, acc_ref, *, nk):
    ki = pl.program_id(3)

    @pl.when(ki == 0)
    def _():
        m_ref[...] = jnp.full(m_ref.shape, -jnp.inf, F32)
        l_ref[...] = jnp.zeros(l_ref.shape, F32)
        acc_ref[...] = jnp.zeros(acc_ref.shape, F32)

    v = v_ref[...]
    for c in range(2):
        q = q_ref[:, c * HEAD_DIM:(c + 1) * HEAD_DIM]
        k = k_ref[:, c * HEAD_DIM:(c + 1) * HEAD_DIM]
        s = lax.dot_general(q, k, (((1,), (1,)), ((), ())), preferred_element_type=F32)
        m_old = m_ref[c]
        m_new = jnp.maximum(m_old, jnp.max(s, axis=1, keepdims=True))
        a = jnp.exp(m_old - m_new)
        p = jnp.exp(s - m_new)
        l_ref[c] = a * l_ref[c] + jnp.sum(p, axis=1, keepdims=True)
        acc_ref[c] = a * acc_ref[c] + jnp.dot(p.astype(v.dtype), v, preferred_element_type=F32)
        m_ref[c] = m_new

    @pl.when(ki == nk - 1)
    def _():
        lam = (jnp.exp(jnp.sum(lq1_ref[...] * lk1_ref[...], axis=1, keepdims=True))
               - jnp.exp(jnp.sum(lq2_ref[...] * lk2_ref[...], axis=1, keepdims=True))
               + LAMBDA_INIT)
        o = acc_ref[0] / l_ref[0] - lam * (acc_ref[1] / l_ref[1])
        ms = jnp.mean(o * o, axis=-1, keepdims=True)
        o = o * lax.rsqrt(ms + LN_EPS) * g_ref[...] * (1.0 - LAMBDA_INIT)
        o_ref[...] = o.astype(o_ref.dtype)


def _diff_attention(u, lq1, lk1, lq2, lk2, subln_g, batch, seq, n_heads):
    m = u.shape[0]
    hw = 2 * HEAD_DIM
    tq = _blk(seq, TILES["attn_tq"], BF16_SUBLANES)
    tk = _blk(seq, TILES["attn_tk"], LANES)
    nq, nk = seq // tq, seq // tk
    vec = lambda a: a.reshape(1, -1).astype(F32)
    small = pl.BlockSpec((1, HEAD_DIM), lambda b, h, qi, ki: (0, 0))
    blk_bytes = (tq + 2 * tk) * hw * 2 + tq * hw * 2
    scratch_bytes = 2 * tq * (hw + 2 * LANES) * 4
    temp_bytes = 3 * tq * tk * 4
    return pl.pallas_call(
        functools.partial(_attn_body, nk=nk),
        grid=(batch, n_heads, nq, nk),
        in_specs=[pl.BlockSpec((tq, hw), lambda b, h, qi, ki: (b * nq + qi, h)),
                  pl.BlockSpec((tk, hw), lambda b, h, qi, ki: (b * nk + ki, n_heads + h)),
                  pl.BlockSpec((tk, hw), lambda b, h, qi, ki: (b * nk + ki, 2 * n_heads + h)),
                  small, small, small, small,
                  pl.BlockSpec((1, hw), lambda b, h, qi, ki: (0, 0))],
        out_specs=pl.BlockSpec((tq, hw), lambda b, h, qi, ki: (b * nq + qi, h)),
        out_shape=jax.ShapeDtypeStruct((m, n_heads * hw), BF16),
        scratch_shapes=[pltpu.VMEM((2, tq, 1), F32), pltpu.VMEM((2, tq, 1), F32),
                        pltpu.VMEM((2, tq, hw), F32)],
        compiler_params=_params(("parallel", "parallel", "parallel", "arbitrary"),
                                _vmem_limit(blk_bytes, scratch_bytes + temp_bytes)),
        name="diff_attn",
    )(u, u, u, vec(lq1), vec(lk1), vec(lq2), vec(lk2), vec(subln_g))


def _dft_consts(seq):
    def cs(n, rows, cols):
        ph = 2.0 * np.pi * ((np.arange(rows)[:, None] * np.arange(cols)[None, :]) % n) / n
        return np.cos(ph), np.sin(ph)

    g = FOURIER_GROUP
    cc, sc = cs(g, g, g)
    w_chan = np.concatenate([cc, -sc], axis=1) / math.sqrt(g)
    na = SEQ_DFT_MAJOR
    nb = seq // na
    ca, sa = cs(na, na, na)
    f_a = np.block([[ca, sa], [-sa, ca]])
    tc, ts = cs(seq, nb, na)
    tw_c = (tc / math.sqrt(seq))[:, :, None]
    tw_s = (ts / math.sqrt(seq))[:, :, None]
    cb, sb = cs(nb, nb, nb)
    f_b = np.concatenate([cb, sb], axis=1)
    as32 = lambda a: jnp.asarray(a.astype(np.float32))
    return as32(w_chan), as32(f_a), as32(tw_c), as32(tw_s), as32(f_b)


def _chan_dft_body(x_ref, w_ref, z_ref):
    r = jnp.dot(x_ref[...], w_ref[...], preferred_element_type=F32)
    g = x_ref.shape[1]
    z_ref[0] = r[:, :g].astype(z_ref.dtype)
    z_ref[1] = r[:, g:].astype(z_ref.dtype)


def _dft_a_body(z_ref, fa_ref, twc_ref, tws_ref, t_ref, *, jb, chans, na):
    fa = fa_ref[...]
    for jj in range(jb):
        cols = slice(jj * chans, (jj + 1) * chans)
        zz = jnp.concatenate([z_ref[0, :, cols], z_ref[1, :, cols]], axis=0)
        r = jnp.dot(fa, zz, preferred_element_type=F32)
        tr, ti = r[:na], r[na:]
        c, s = twc_ref[jj], tws_ref[jj]
        t_ref[0, jj] = (tr * c + ti * s).astype(t_ref.dtype)
        t_ref[1, jj] = (ti * c - tr * s).astype(t_ref.dtype)


def _dft_b_body(t_ref, fb_ref, y_ref):
    y_ref[...] = jnp.dot(fb_ref[...], t_ref[...], preferred_element_type=F32).astype(y_ref.dtype)


def _fourier_mix(u, batch, seq, f_col0, chans):
    m = u.shape[0]
    g = FOURIER_GROUP
    n_groups = chans // g
    w_chan, f_a, tw_c, tw_s, f_b = _dft_consts(seq)
    na, nb = SEQ_DFT_MAJOR, seq // SEQ_DFT_MAJOR

    bm = _blk(m, TILES["chan_bm"], BF16_SUBLANES)
    z = pl.pallas_call(
        _chan_dft_body,
        grid=(m // bm, n_groups),
        in_specs=[pl.BlockSpec((bm, g), lambda i, gi: (i, f_col0 // g + gi)),
                  pl.BlockSpec((g, 2 * g), lambda i, gi: (0, 0))],
        out_specs=pl.BlockSpec((2, bm, g), lambda i, gi: (0, i, gi)),
        out_shape=jax.ShapeDtypeStruct((2, m, chans), BF16),
        compiler_params=_params(("parallel", "parallel"), _vmem_limit(3 * bm * g * 2, 2 * bm * g * 4)),
        name="chan_dft",
    )(u, w_chan.astype(BF16))

    jb = _blk(nb, TILES["dft_a_jb"], 1)
    z4 = z.reshape(2, batch, na, nb * chans)
    t = pl.pallas_call(
        functools.partial(_dft_a_body, jb=jb, chans=chans, na=na),
        grid=(batch, nb // jb),
        in_specs=[pl.BlockSpec((2, None, na, jb * chans), lambda b, j: (0, b, 0, j)),
                  pl.BlockSpec((2 * na, 2 * na), lambda b, j: (0, 0)),
                  pl.BlockSpec((jb, na, 1), lambda b, j: (j, 0, 0)),
                  pl.BlockSpec((jb, na, 1), lambda b, j: (j, 0, 0))],
        out_specs=pl.BlockSpec((None, 2, jb, na, chans), lambda b, j: (b, 0, j, 0, 0)),
        out_shape=jax.ShapeDtypeStruct((batch, 2, nb, na, chans), BF16),
        compiler_params=_params(("parallel", "parallel"),
                                _vmem_limit(4 * na * jb * chans * 2 + 2 * jb * na * LANES * 4,
                                            6 * na * chans * 4)),
        name="seq_dft_a",
    )(z4, f_a.astype(BF16), tw_c, tw_s)

    t2 = t.reshape(batch, 2 * nb, na * chans)
    bc = _blk(na * chans, TILES["dft_b_cols"], LANES)
    y = pl.pallas_call(
        _dft_b_body,
        grid=(batch, na * chans // bc),
        in_specs=[pl.BlockSpec((None, 2 * nb, bc), lambda b, j: (b, 0, j)),
                  pl.BlockSpec((nb, 2 * nb), lambda b, j: (0, 0))],
        out_specs=pl.BlockSpec((None, nb, bc), lambda b, j: (b, 0, j)),
        out_shape=jax.ShapeDtypeStruct((batch, nb, na * chans), BF16),
        compiler_params=_params(("parallel", "parallel"), _vmem_limit(3 * nb * bc * 2, nb * bc * 4)),
        name="seq_dft_b",
    )(t2, f_b.astype(BF16))
    return y.reshape(m, chans)


def _sigmoid(x):
    return 1.0 / (1.0 + jnp.exp(-x))


def _gate_mix_body(h_ref, a_ref, f_ref, wga_ref, wgf_ref, bga_ref, bgf_ref, wao_ref, wfo_ref, o_ref):
    h = h_ref[...]
    ga = _sigmoid(jnp.dot(h, wga_ref[...], preferred_element_type=F32) + bga_ref[...])
    gf = _sigmoid(jnp.dot(h, wgf_ref[...], preferred_element_type=F32) + bgf_ref[...])
    ya = jnp.dot(a_ref[...], wao_ref[...], preferred_element_type=F32)
    yf = jnp.dot(f_ref[...], wfo_ref[...], preferred_element_type=F32)
    o_ref[...] = (ga * ya + gf * yf).astype(o_ref.dtype)


def _gate_mix(h0b, attn, four, w_gate_b, b_gate, w_ao_b, w_fo_b):
    m, d = h0b.shape
    ka, kf = attn.shape[1], four.shape[1]
    bm = _blk(m, TILES["gate_bm"], BF16_SUBLANES)
    bn = _blk(d, TILES["gate_bn"], LANES)
    nb = d // bn
    bg = b_gate.reshape(1, 2 * d).astype(F32)
    blk_bytes = (bm * (d + ka + kf) + (2 * d + ka + kf) * bn + bm * bn) * 2
    return pl.pallas_call(
        _gate_mix_body,
        grid=(m // bm, nb),
        in_specs=[pl.BlockSpec((bm, d), lambda i, j: (i, 0)),
                  pl.BlockSpec((bm, ka), lambda i, j: (i, 0)),
                  pl.BlockSpec((bm, kf), lambda i, j: (i, 0)),
                  pl.BlockSpec((d, bn), lambda i, j: (0, j)),
                  pl.BlockSpec((d, bn), lambda i, j: (0, nb + j)),
                  pl.BlockSpec((1, bn), lambda i, j: (0, j)),
                  pl.BlockSpec((1, bn), lambda i, j: (0, nb + j)),
                  pl.BlockSpec((ka, bn), lambda i, j: (0, j)),
                  pl.BlockSpec((kf, bn), lambda i, j: (0, j))],
        out_specs=pl.BlockSpec((bm, bn), lambda i, j: (i, j)),
        out_shape=jax.ShapeDtypeStruct((m, d), BF16),
        compiler_params=_params(("parallel", "arbitrary"), _vmem_limit(blk_bytes, 6 * bm * bn * 4)),
        name="gate_mix",
    )(h0b, attn, four, w_gate_b, w_gate_b, bg, bg, w_ao_b, w_fo_b)


def _mm_res_body(x_ref, w_ref, r_ref, o_ref):
    o_ref[...] = ALPHA * r_ref[...] + jnp.dot(x_ref[...], w_ref[...], preferred_element_type=F32)


def _matmul_residual(x, w, res, bm_pref, bn_pref):
    m, k = x.shape
    n = w.shape[1]
    bm = _blk(m, bm_pref, BF16_SUBLANES)
    bn = _blk(n, bn_pref, LANES)
    blk_bytes = (bm * k + k * bn) * 2 + 2 * bm * bn * 4
    return pl.pallas_call(
        _mm_res_body,
        grid=(m // bm, n // bn),
        in_specs=[pl.BlockSpec((bm, k), lambda i, j: (i, 0)),
                  pl.BlockSpec((k, bn), lambda i, j: (0, j)),
                  pl.BlockSpec((bm, bn), lambda i, j: (i, j))],
        out_specs=pl.BlockSpec((bm, bn), lambda i, j: (i, j)),
        out_shape=jax.ShapeDtypeStruct((m, n), F32),
        compiler_params=_params(("parallel", "arbitrary"), _vmem_limit(blk_bytes, bm * bn * 4)),
        name="matmul_residual",
    )(x, w, res)


def _up_body(x_ref, xp_ref, xn_ref, wg_ref, wv_ref, cwg_ref, cwv_ref, cbg_ref, cbv_ref, o_ref, xs_ref,
             *, bm, blocks_per_seq):
    i = pl.program_id(0)
    j = pl.program_id(1)
    rows = bm + 2 * HALO

    @pl.when(j == 0)
    def _():
        pos = i % blocks_per_seq
        xs_ref[0:bm, :] = x_ref[...]
        xs_ref[bm:bm + HALO, :] = jnp.where(pos != blocks_per_seq - 1, xn_ref[...], jnp.zeros_like(xn_ref))
        xs_ref[bm + HALO:rows, :] = jnp.where(pos != 0, xp_ref[...], jnp.zeros_like(xp_ref))

    xs = xs_ref[...]

    def conv(w_ref, cw_ref, cb_ref):
        a = jnp.dot(xs, w_ref[...], preferred_element_type=F32)
        prev = pltpu.roll(a, 1, 0)[0:bm]
        nxt = pltpu.roll(a, rows - 1, 0)[0:bm]
        cw = cw_ref[...]
        return prev * cw[0:1] + a[0:bm] * cw[1:2] + nxt * cw[2:3] + cb_ref[...]

    gate = conv(wg_ref, cwg_ref, cbg_ref)
    val = conv(wv_ref, cwv_ref, cbv_ref)
    o_ref[...] = (gate * _sigmoid(gate) * val).astype(o_ref.dtype)


def _conv_ffn_up(h1b, w_up_b, conv_w, conv_b, seq):
    m, d = h1b.shape
    dff = w_up_b.shape[1] // 2
    bm = _blk(seq, TILES["up_bm"], HALO)
    bn = _blk(dff, TILES["up_bn"], LANES)
    nb = dff // bn
    hb = bm // HALO
    last = m // HALO - 1
    cw = conv_w.astype(F32)
    cb = conv_b.reshape(1, 2 * dff).astype(F32)
    body = functools.partial(_up_body, bm=bm, blocks_per_seq=seq // bm)
    blk_bytes = (bm + 2 * HALO) * d * 2 + 2 * d * bn * 2 + bm * bn * 2 + 8 * bn * 4
    return pl.pallas_call(
        body,
        grid=(m // bm, nb),
        in_specs=[pl.BlockSpec((bm, d), lambda i, j: (i, 0)),
                  pl.BlockSpec((HALO, d), lambda i, j: (jnp.maximum(i * hb - 1, 0), 0)),
                  pl.BlockSpec((HALO, d), lambda i, j: (jnp.minimum((i + 1) * hb, last), 0)),
                  pl.BlockSpec((d, bn), lambda i, j: (0, j)),
                  pl.BlockSpec((d, bn), lambda i, j: (0, nb + j)),
                  pl.BlockSpec((CONV_WIDTH, bn), lambda i, j: (0, j)),
                  pl.BlockSpec((CONV_WIDTH, bn), lambda i, j: (0, nb + j)),
                  pl.BlockSpec((1, bn), lambda i, j: (0, j)),
                  pl.BlockSpec((1, bn), lambda i, j: (0, nb + j))],
        out_specs=pl.BlockSpec((bm, bn), lambda i, j: (i, j)),
        out_shape=jax.ShapeDtypeStruct((m, dff), BF16),
        scratch_shapes=[pltpu.VMEM((bm + 2 * HALO, d), BF16)],
        compiler_params=_params(("parallel", "arbitrary"),
                                _vmem_limit(blk_bytes, (bm + 2 * HALO) * (d * 2 + 8 * bn * 4))),
        name="conv_ffn_up",
    )(h1b, h1b, h1b, w_up_b, w_up_b, cw, cw, cb, cb)


def kernel(x, ln_emb_g, ln_emb_b, w_in, lambda_q1, lambda_k1, lambda_q2, lambda_k2, subln_g, w_attn_o, w_fourier,
           w_gate, b_gate, w_mix_out, ln1_g, ln1_b, w_up, conv_w, conv_b, w_down, ln2_g, ln2_b):
    batch, seq, d = x.shape
    m = batch * seq
    depth = w_in.shape[0]
    assert depth == DEPTH
    attn_v = w_attn_o.shape[1]
    chans = w_fourier.shape[1]
    qk_cols = (w_in.shape[2] - attn_v - chans) // 2
    n_heads = attn_v // (2 * HEAD_DIM)
    assert qk_cols == attn_v and seq % SEQ_DFT_MAJOR == 0 and chans % FOURIER_GROUP == 0

    (h0, h0b) = _layer_norm(x.reshape(m, d), ln_emb_g, ln_emb_b, (F32, BF16))
    l = 0
    u = _in_proj(h0b, w_in[l].astype(BF16), seq, qk_cols)
    attn = _diff_attention(u, lambda_q1[l], lambda_k1[l], lambda_q2[l], lambda_k2[l], subln_g[l],
                           batch, seq, n_heads)
    four = _fourier_mix(u, batch, seq, 2 * qk_cols + attn_v, chans)
    mix = _gate_mix(h0b, attn, four, w_gate[l].astype(BF16), b_gate[l],
                    w_attn_o[l].astype(BF16), w_fourier[l].astype(BF16))
    s1 = _matmul_residual(mix, w_mix_out[l].astype(BF16), h0, TILES["mix_bm"], TILES["mix_bn"])
    (h1, h1b) = _layer_norm(s1, ln1_g[l], ln1_b[l], (F32, BF16))
    act = _conv_ffn_up(h1b, w_up[l].astype(BF16), conv_w[l], conv_b[l], seq)
    s2 = _matmul_residual(act, w_down[l].astype(BF16), h1, TILES["down_bm"], TILES["down_bn"])
    (out,) = _layer_norm(s2, ln2_g[l], ln2_b[l], (F32,))
    return out.reshape(batch, seq, d)
```

```python
import functools
import math

import numpy as np
import jax
import jax.numpy as jnp
from jax import lax
from jax.experimental import pallas as pl
from jax.experimental.pallas import tpu as pltpu

F32 = jnp.float32
BF16 = jnp.bfloat16

HEAD_DIM = 128
ROT_DIM = HEAD_DIM // 4
ROPE_THETA = 500000.0
FOURIER_GROUP = 256
CONV_WIDTH = 3
DEPTH = 1
ALPHA = (2.0 * DEPTH) ** 0.25
LN_EPS = 1e-5
LAMBDA_INIT = 0.8 - 0.6 * math.exp(-0.3 * 0)

LANES = 128
BF16_SUBLANES = 16
V7X_VMEM_BYTES = 64 * 1024 * 1024
VMEM_CAP_BYTES = 60000 * 1024
SEQ_DFT_MAJOR = 128
HALO = BF16_SUBLANES

TILES = dict(ln_rows=256, in_bm=1024, in_bn=1024, attn_tq=2048, attn_rows=256, attn_tk=1024, chan_bm=2048, dft_a_jb=4,
             dft_b_cols=16384, gate_bm=512, gate_bn=512, mix_bm=1024, mix_bn=512, up_bm=1024, up_bn=512,
             up_chunk=256, down_bm=512, down_bn=256)
FFN_PAD_MULTIPLE = 1024


def _vmem_limit(pipelined_bytes, resident_bytes=0):
    need = 2 * pipelined_bytes + resident_bytes + (2 << 20)
    return int(min(max(need, 16 << 20), VMEM_CAP_BYTES))


def _blk(n, pref, align):
    if n <= pref:
        return n
    b = (pref // align) * align
    while b >= align:
        if n % b == 0:
            return b
        b -= align
    raise ValueError(f"no block of alignment {align} divides {n}")


def _params(sem, vmem):
    return pltpu.CompilerParams(dimension_semantics=sem, vmem_limit_bytes=vmem)


def _ln_math(x, g, b):
    mu = jnp.mean(x, axis=-1, keepdims=True)
    xc = x - mu
    var = jnp.mean(xc * xc, axis=-1, keepdims=True)
    return xc * lax.rsqrt(var + LN_EPS) * g + b


def _ln_body(x_ref, g_ref, b_ref, *o_refs):
    y = _ln_math(x_ref[...].astype(F32), g_ref[...], b_ref[...])
    for o_ref in o_refs:
        o_ref[...] = y.astype(o_ref.dtype)


def _layer_norm(x, g, b, out_dtypes):
    m, d = x.shape
    bm = _blk(m, TILES["ln_rows"], BF16_SUBLANES)
    row = pl.BlockSpec((bm, d), lambda i: (i, 0))
    vec = pl.BlockSpec((1, d), lambda i: (0, 0))
    blk_bytes = bm * d * (4 + sum(jnp.dtype(t).itemsize for t in out_dtypes))
    return pl.pallas_call(
        _ln_body,
        grid=(m // bm,),
        in_specs=[row, vec, vec],
        out_specs=[row] * len(out_dtypes),
        out_shape=[jax.ShapeDtypeStruct((m, d), t) for t in out_dtypes],
        compiler_params=_params(("parallel",), _vmem_limit(blk_bytes, 4 * bm * d * 4)),
        name="layer_norm",
    )(x, g.reshape(1, d).astype(F32), b.reshape(1, d).astype(F32))


def _in_proj_body(x_ref, w_ref, cos_ref, sin_ref, o_ref, *, n_q_blocks, n_rot_blocks, q_scale):
    j = pl.program_id(1)
    acc = jnp.dot(x_ref[...], w_ref[...], preferred_element_type=F32)
    bm, bn = acc.shape

    @pl.when(j < n_rot_blocks)
    def _():
        c = cos_ref[...]
        s = sin_ref[...]
        scale = jnp.where(j < n_q_blocks, q_scale, 1.0).astype(F32)
        lane = lax.broadcasted_iota(jnp.int32, (bm, LANES), 1)
        first_half = lane < (ROT_DIM // 2)
        for cb in range(bn // LANES):
            t = acc[:, cb * LANES:(cb + 1) * LANES]
            partner = jnp.where(first_half,
                                pltpu.roll(t, LANES - ROT_DIM // 2, 1),
                                pltpu.roll(t, ROT_DIM // 2, 1))
            o_ref[:, cb * LANES:(cb + 1) * LANES] = ((t * c + partner * s) * scale).astype(o_ref.dtype)

    @pl.when(j >= n_rot_blocks)
    def _():
        o_ref[...] = acc.astype(o_ref.dtype)


def _rotary_tables(seq):
    inv_freq = ROPE_THETA ** (-jnp.arange(0, ROT_DIM, 2, dtype=F32) / ROT_DIM)
    ang = jnp.arange(seq, dtype=F32)[:, None] * inv_freq[None, :]
    cos, sin = jnp.cos(ang), jnp.sin(ang)
    pad = HEAD_DIM - ROT_DIM
    cos_t = jnp.concatenate([cos, cos, jnp.ones((seq, pad), F32)], axis=1)
    sin_t = jnp.concatenate([-sin, sin, jnp.zeros((seq, pad), F32)], axis=1)
    return cos_t, sin_t


def _in_proj(h0b, w_in_b, seq, qk_cols):
    m, d = h0b.shape
    n = w_in_b.shape[1]
    bm = _blk(seq, TILES["in_bm"], BF16_SUBLANES)
    bn = _blk(qk_cols, TILES["in_bn"], LANES)
    cos_t, sin_t = _rotary_tables(seq)
    tab = pl.BlockSpec((bm, LANES), lambda i, j: (i % (seq // bm), 0))
    body = functools.partial(_in_proj_body, n_q_blocks=qk_cols // bn, n_rot_blocks=2 * qk_cols // bn,
                             q_scale=HEAD_DIM ** -0.5 * math.log2(math.e))
    blk_bytes = bm * d * 2 + d * bn * 2 + bm * bn * 2 + 2 * bm * LANES * 4
    return pl.pallas_call(
        body,
        grid=(m // bm, n // bn),
        in_specs=[pl.BlockSpec((bm, d), lambda i, j: (i, 0)),
                  pl.BlockSpec((d, bn), lambda i, j: (0, j)),
                  tab, tab],
        out_specs=pl.BlockSpec((bm, bn), lambda i, j: (i, j)),
        out_shape=jax.ShapeDtypeStruct((m, n), BF16),
        compiler_params=_params(("parallel", "arbitrary"), _vmem_limit(blk_bytes, 2 * bm * bn * 4)),
        name="in_proj",
    )(h0b, w_in_b, cos_t, sin_t)


def _attn_body(q_ref, k_ref, v_ref, lq1_ref, lk1_ref, lq2_ref, lk2_ref, g_ref, o_ref,
               s0_ref, s1_ref, p0_ref, p1_ref, a0_ref, a1_ref, m_ref, l_ref, acc_ref, *, rows, tk, nr, nt):
    s_refs, p_refs, a_refs = (s0_ref, s1_ref), (p0_ref, p1_ref), (a0_ref, a1_ref)
    n_items = nr * nt
    m_ref[...] = jnp.full(m_ref.shape, -jnp.inf, F32)
    l_ref[...] = jnp.zeros(l_ref.shape, F32)
    acc_ref[...] = jnp.zeros(acc_ref.shape, F32)

    def offsets(i):
        if isinstance(i, int):
            return (i % nr) * rows, (i // nr) * tk
        return pl.multiple_of(lax.rem(i, nr) * rows, rows), pl.multiple_of(lax.div(i, nr) * tk, tk)

    def qk(i, slot):
        row0, key0 = offsets(i)
        for c in range(2):
            cols = slice(c * HEAD_DIM, (c + 1) * HEAD_DIM)
            s_refs[slot][c] = lax.dot_general(q_ref[pl.ds(row0, rows), cols], k_ref[pl.ds(key0, tk), cols],
                                              (((1,), (1,)), ((), ())), preferred_element_type=F32)

    def softmax(i, slot):
        row0, _ = offsets(i)
        for c in range(2):
            s = s_refs[slot][c]
            m_old = m_ref[c, pl.ds(row0, rows), :]
            m_new = jnp.maximum(m_old, jnp.max(s, axis=1, keepdims=True))
            a = jnp.exp2(m_old - m_new)
            p = jnp.exp2(s - m_new)
            l_ref[c, pl.ds(row0, rows), :] = a * l_ref[c, pl.ds(row0, rows), :] + jnp.sum(p, axis=1, keepdims=True)
            m_ref[c, pl.ds(row0, rows), :] = m_new
            p_refs[slot][c] = p.astype(p_refs[slot].dtype)
            a_refs[slot][c] = a

    def pv(i, slot):
        row0, key0 = offsets(i)
        v = v_ref[pl.ds(key0, tk), :]
        for c in range(2):
            acc_ref[c, pl.ds(row0, rows), :] = (a_refs[slot][c] * acc_ref[c, pl.ds(row0, rows), :]
                                                + jnp.dot(p_refs[slot][c], v, preferred_element_type=F32))

    qk(0, 0)
    qk(1, 1)
    softmax(0, 0)

    def pair(j, carry):
        i0 = 2 * j
        pv(i0 - 2, 0)
        qk(i0, 0)
        softmax(i0 - 1, 1)
        pv(i0 - 1, 1)
        qk(i0 + 1, 1)
        softmax(i0, 0)
        return carry

    lax.fori_loop(1, n_items // 2, pair, 0)
    pv(n_items - 2, 0)
    softmax(n_items - 1, 1)
    pv(n_items - 1, 1)

    lam = (jnp.exp(jnp.sum(lq1_ref[...] * lk1_ref[...], axis=1, keepdims=True))
           - jnp.exp(jnp.sum(lq2_ref[...] * lk2_ref[...], axis=1, keepdims=True))
           + LAMBDA_INIT)
    o = acc_ref[0] / l_ref[0] - lam * (acc_ref[1] / l_ref[1])
    ms = jnp.mean(o * o, axis=-1, keepdims=True)
    o = o * lax.rsqrt(ms + LN_EPS) * g_ref[...] * (1.0 - LAMBDA_INIT)
    o_ref[...] = o.astype(o_ref.dtype)


def _diff_attention(u, lq1, lk1, lq2, lk2, subln_g, batch, seq, n_heads):
    m = u.shape[0]
    hw = 2 * HEAD_DIM
    tq = _blk(seq, TILES["attn_tq"], BF16_SUBLANES)
    rows = _blk(tq, TILES["attn_rows"], BF16_SUBLANES)
    tk = _blk(seq, TILES["attn_tk"], LANES)
    nq, nr, nt = seq // tq, tq // rows, seq // tk
    assert (nr * nt) % 2 == 0 and nr * nt >= 4 and nr >= 2
    vec = lambda a: a.reshape(1, -1).astype(F32)
    small = pl.BlockSpec((1, HEAD_DIM), lambda b, h, qi: (0, 0))
    blk_bytes = 2 * tq * hw * 2 + 2 * seq * hw * 2
    scratch_bytes = 4 * rows * tk * (4 + 2) + 4 * rows * LANES * 4 + 2 * tq * (hw + 2 * LANES) * 4
    return pl.pallas_call(
        functools.partial(_attn_body, rows=rows, tk=tk, nr=nr, nt=nt),
        grid=(batch, n_heads, nq),
        in_specs=[pl.BlockSpec((tq, hw), lambda b, h, qi: (b * nq + qi, h)),
                  pl.BlockSpec((seq, hw), lambda b, h, qi: (b, n_heads + h)),
                  pl.BlockSpec((seq, hw), lambda b, h, qi: (b, 2 * n_heads + h)),
                  small, small, small, small,
                  pl.BlockSpec((1, hw), lambda b, h, qi: (0, 0))],
        out_specs=pl.BlockSpec((tq, hw), lambda b, h, qi: (b * nq + qi, h)),
        out_shape=jax.ShapeDtypeStruct((m, n_heads * hw), BF16),
        scratch_shapes=[pltpu.VMEM((2, rows, tk), F32), pltpu.VMEM((2, rows, tk), F32),
                        pltpu.VMEM((2, rows, tk), BF16), pltpu.VMEM((2, rows, tk), BF16),
                        pltpu.VMEM((2, rows, 1), F32), pltpu.VMEM((2, rows, 1), F32),
                        pltpu.VMEM((2, tq, 1), F32), pltpu.VMEM((2, tq, 1), F32),
                        pltpu.VMEM((2, tq, hw), F32)],
        compiler_params=_params(("parallel", "parallel", "arbitrary"),
                                _vmem_limit(blk_bytes, scratch_bytes + 2 * rows * tk * 4 + 6 * tq * hw * 4)),
        name="diff_attn",
    )(u, u, u, vec(lq1), vec(lk1), vec(lq2), vec(lk2), vec(subln_g))


def _dft_consts(seq):
    def cs(n, rows, cols):
        ph = 2.0 * np.pi * ((np.arange(rows)[:, None] * np.arange(cols)[None, :]) % n) / n
        return np.cos(ph), np.sin(ph)

    g = FOURIER_GROUP
    cc, sc = cs(g, g, g)
    w_chan = np.concatenate([cc, -sc], axis=1) / math.sqrt(g)
    na = SEQ_DFT_MAJOR
    nb = seq // na
    ca, sa = cs(na, na, na)
    f_a = np.block([[ca, sa], [-sa, ca]])
    tc, ts = cs(seq, nb, na)
    tw_c = (tc / math.sqrt(seq))[:, :, None]
    tw_s = (ts / math.sqrt(seq))[:, :, None]
    cb, sb = cs(nb, nb, nb)
    f_b = np.concatenate([cb, sb], axis=1)
    as32 = lambda a: jnp.asarray(a.astype(np.float32))
    return as32(w_chan), as32(f_a), as32(tw_c), as32(tw_s), as32(f_b)


def _chan_dft_body(x_ref, w_ref, z_ref):
    r = jnp.dot(x_ref[...], w_ref[...], preferred_element_type=F32)
    g = x_ref.shape[1]
    z_ref[0] = r[:, :g].astype(z_ref.dtype)
    z_ref[1] = r[:, g:].astype(z_ref.dtype)


def _dft_a_body(z_ref, fa_ref, twc_ref, tws_ref, t_ref, *, jb, chans, na):
    fa = fa_ref[...]
    for jj in range(jb):
        cols = slice(jj * chans, (jj + 1) * chans)
        zz = jnp.concatenate([z_ref[0, :, cols], z_ref[1, :, cols]], axis=0)
        r = jnp.dot(fa, zz, preferred_element_type=F32)
        tr, ti = r[:na], r[na:]
        c, s = twc_ref[jj], tws_ref[jj]
        t_ref[0, jj] = (tr * c + ti * s).astype(t_ref.dtype)
        t_ref[1, jj] = (ti * c - tr * s).astype(t_ref.dtype)


def _dft_b_body(t_ref, fb_ref, y_ref):
    y_ref[...] = jnp.dot(fb_ref[...], t_ref[...], preferred_element_type=F32).astype(y_ref.dtype)


def _fourier_mix(u, batch, seq, f_col0, chans):
    m = u.shape[0]
    g = FOURIER_GROUP
    n_groups = chans // g
    w_chan, f_a, tw_c, tw_s, f_b = _dft_consts(seq)
    na, nb = SEQ_DFT_MAJOR, seq // SEQ_DFT_MAJOR

    bm = _blk(m, TILES["chan_bm"], BF16_SUBLANES)
    z = pl.pallas_call(
        _chan_dft_body,
        grid=(m // bm, n_groups),
        in_specs=[pl.BlockSpec((bm, g), lambda i, gi: (i, f_col0 // g + gi)),
                  pl.BlockSpec((g, 2 * g), lambda i, gi: (0, 0))],
        out_specs=pl.BlockSpec((2, bm, g), lambda i, gi: (0, i, gi)),
        out_shape=jax.ShapeDtypeStruct((2, m, chans), BF16),
        compiler_params=_params(("parallel", "parallel"), _vmem_limit(3 * bm * g * 2, 2 * bm * g * 4)),
        name="chan_dft",
    )(u, w_chan.astype(BF16))

    jb = _blk(nb, TILES["dft_a_jb"], 1)
    z4 = z.reshape(2, batch, na, nb * chans)
    t = pl.pallas_call(
        functools.partial(_dft_a_body, jb=jb, chans=chans, na=na),
        grid=(batch, nb // jb),
        in_specs=[pl.BlockSpec((2, None, na, jb * chans), lambda b, j: (0, b, 0, j)),
                  pl.BlockSpec((2 * na, 2 * na), lambda b, j: (0, 0)),
                  pl.BlockSpec((jb, na, 1), lambda b, j: (j, 0, 0)),
                  pl.BlockSpec((jb, na, 1), lambda b, j: (j, 0, 0))],
        out_specs=pl.BlockSpec((None, 2, jb, na, chans), lambda b, j: (b, 0, j, 0, 0)),
        out_shape=jax.ShapeDtypeStruct((batch, 2, nb, na, chans), BF16),
        compiler_params=_params(("parallel", "parallel"),
                                _vmem_limit(4 * na * jb * chans * 2 + 2 * jb * na * LANES * 4,
                                            6 * na * chans * 4)),
        name="seq_dft_a",
    )(z4, f_a.astype(BF16), tw_c, tw_s)

    t2 = t.reshape(batch, 2 * nb, na * chans)
    bc = _blk(na * chans, TILES["dft_b_cols"], LANES)
    y = pl.pallas_call(
        _dft_b_body,
        grid=(batch, na * chans // bc),
        in_specs=[pl.BlockSpec((None, 2 * nb, bc), lambda b, j: (b, 0, j)),
                  pl.BlockSpec((nb, 2 * nb), lambda b, j: (0, 0))],
        out_specs=pl.BlockSpec((None, nb, bc), lambda b, j: (b, 0, j)),
        out_shape=jax.ShapeDtypeStruct((batch, nb, na * chans), BF16),
        compiler_params=_params(("parallel", "parallel"), _vmem_limit(3 * nb * bc * 2, nb * bc * 4)),
        name="seq_dft_b",
    )(t2, f_b.astype(BF16))
    return y.reshape(m, chans)


def _sigmoid(x):
    return 1.0 / (1.0 + jnp.exp(-x))


def _gate_mix_body(h_ref, a_ref, f_ref, wga_ref, wgf_ref, bga_ref, bgf_ref, wao_ref, wfo_ref, o_ref):
    h = h_ref[...]
    ga = _sigmoid(jnp.dot(h, wga_ref[...], preferred_element_type=F32) + bga_ref[...])
    gf = _sigmoid(jnp.dot(h, wgf_ref[...], preferred_element_type=F32) + bgf_ref[...])
    ya = jnp.dot(a_ref[...], wao_ref[...], preferred_element_type=F32)
    yf = jnp.dot(f_ref[...], wfo_ref[...], preferred_element_type=F32)
    o_ref[...] = (ga * ya + gf * yf).astype(o_ref.dtype)


def _gate_mix(h0b, attn, four, w_gate_b, b_gate, w_ao_b, w_fo_b):
    m, d = h0b.shape
    ka, kf = attn.shape[1], four.shape[1]
    bm = _blk(m, TILES["gate_bm"], BF16_SUBLANES)
    bn = _blk(d, TILES["gate_bn"], LANES)
    nb = d // bn
    bg = b_gate.reshape(1, 2 * d).astype(F32)
    blk_bytes = (bm * (d + ka + kf) + (2 * d + ka + kf) * bn + bm * bn) * 2
    return pl.pallas_call(
        _gate_mix_body,
        grid=(m // bm, nb),
        in_specs=[pl.BlockSpec((bm, d), lambda i, j: (i, 0)),
                  pl.BlockSpec((bm, ka), lambda i, j: (i, 0)),
                  pl.BlockSpec((bm, kf), lambda i, j: (i, 0)),
                  pl.BlockSpec((d, bn), lambda i, j: (0, j)),
                  pl.BlockSpec((d, bn), lambda i, j: (0, nb + j)),
                  pl.BlockSpec((1, bn), lambda i, j: (0, j)),
                  pl.BlockSpec((1, bn), lambda i, j: (0, nb + j)),
                  pl.BlockSpec((ka, bn), lambda i, j: (0, j)),
                  pl.BlockSpec((kf, bn), lambda i, j: (0, j))],
        out_specs=pl.BlockSpec((bm, bn), lambda i, j: (i, j)),
        out_shape=jax.ShapeDtypeStruct((m, d), BF16),
        compiler_params=_params(("parallel", "arbitrary"), _vmem_limit(blk_bytes, 6 * bm * bn * 4)),
        name="gate_mix",
    )(h0b, attn, four, w_gate_b, w_gate_b, bg, bg, w_ao_b, w_fo_b)


def _mm_res_body(x_ref, w_ref, r_ref, o_ref):
    o_ref[...] = ALPHA * r_ref[...] + jnp.dot(x_ref[...], w_ref[...], preferred_element_type=F32)


def _matmul_residual(x, w, res, bm_pref, bn_pref):
    m, k = x.shape
    n = w.shape[1]
    bm = _blk(m, bm_pref, BF16_SUBLANES)
    bn = _blk(n, bn_pref, LANES)
    blk_bytes = (bm * k + k * bn) * 2 + 2 * bm * bn * 4
    return pl.pallas_call(
        _mm_res_body,
        grid=(m // bm, n // bn),
        in_specs=[pl.BlockSpec((bm, k), lambda i, j: (i, 0)),
                  pl.BlockSpec((k, bn), lambda i, j: (0, j)),
                  pl.BlockSpec((bm, bn), lambda i, j: (i, j))],
        out_specs=pl.BlockSpec((bm, bn), lambda i, j: (i, j)),
        out_shape=jax.ShapeDtypeStruct((m, n), F32),
        compiler_params=_params(("parallel", "arbitrary"), _vmem_limit(blk_bytes, bm * bn * 4)),
        name="matmul_residual",
    )(x, w, res)


def _up_body(x_ref, xp_ref, xn_ref, wg_ref, wv_ref, cwg_ref, cwv_ref, cbg_ref, cbv_ref, o_ref, xs_ref,
             *, bm, blocks_per_seq, chunk):
    i = pl.program_id(0)
    j = pl.program_id(1)
    rows = bm + 2 * HALO

    @pl.when(j == 0)
    def _():
        pos = i % blocks_per_seq
        xs_ref[0:bm, :] = x_ref[...]
        xs_ref[bm:bm + HALO, :] = jnp.where(pos != blocks_per_seq - 1, xn_ref[...], jnp.zeros_like(xn_ref))
        xs_ref[bm + HALO:rows, :] = jnp.where(pos != 0, xp_ref[...], jnp.zeros_like(xp_ref))

    xs = xs_ref[...]

    def conv(w_ref, cw_ref, cb_ref, cols):
        a = jnp.dot(xs, w_ref[:, cols], preferred_element_type=F32)
        prev = pltpu.roll(a, 1, 0)[0:bm]
        nxt = pltpu.roll(a, rows - 1, 0)[0:bm]
        cw = cw_ref[:, cols]
        return prev * cw[0:1] + a[0:bm] * cw[1:2] + nxt * cw[2:3] + cb_ref[:, cols]

    for ch in range(o_ref.shape[1] // chunk):
        cols = slice(ch * chunk, (ch + 1) * chunk)
        gate = conv(wg_ref, cwg_ref, cbg_ref, cols)
        val = conv(wv_ref, cwv_ref, cbv_ref, cols)
        o_ref[:, cols] = (gate * _sigmoid(gate) * val).astype(o_ref.dtype)


def _conv_ffn_up(h1b, w_up_b, conv_w, conv_b, seq):
    m, d = h1b.shape
    dff = w_up_b.shape[1] // 2
    bm = _blk(seq, TILES["up_bm"], HALO)
    bn = _blk(dff, TILES["up_bn"], LANES)
    nb = dff // bn
    hb = bm // HALO
    last = m // HALO - 1
    cw = conv_w.astype(F32)
    cb = conv_b.reshape(1, 2 * dff).astype(F32)
    chunk = _blk(bn, TILES["up_chunk"], LANES)
    body = functools.partial(_up_body, bm=bm, blocks_per_seq=seq // bm, chunk=chunk)
    blk_bytes = (bm + 2 * HALO) * d * 2 + 2 * d * bn * 2 + bm * bn * 2 + 8 * bn * 4
    return pl.pallas_call(
        body,
        grid=(m // bm, nb),
        in_specs=[pl.BlockSpec((bm, d), lambda i, j: (i, 0)),
                  pl.BlockSpec((HALO, d), lambda i, j: (jnp.maximum(i * hb - 1, 0), 0)),
                  pl.BlockSpec((HALO, d), lambda i, j: (jnp.minimum((i + 1) * hb, last), 0)),
                  pl.BlockSpec((d, bn), lambda i, j: (0, j)),
                  pl.BlockSpec((d, bn), lambda i, j: (0, nb + j)),
                  pl.BlockSpec((CONV_WIDTH, bn), lambda i, j: (0, j)),
                  pl.BlockSpec((CONV_WIDTH, bn), lambda i, j: (0, nb + j)),
                  pl.BlockSpec((1, bn), lambda i, j: (0, j)),
                  pl.BlockSpec((1, bn), lambda i, j: (0, nb + j))],
        out_specs=pl.BlockSpec((bm, bn), lambda i, j: (i, j)),
        out_shape=jax.ShapeDtypeStruct((m, dff), BF16),
        scratch_shapes=[pltpu.VMEM((bm + 2 * HALO, d), BF16)],
        compiler_params=_params(("parallel", "arbitrary"),
                                _vmem_limit(blk_bytes, (bm + 2 * HALO) * (d * 2 + 16 * chunk * 4))),
        name="conv_ffn_up",
    )(h1b, h1b, h1b, w_up_b, w_up_b, cw, cw, cb, cb)


def kernel(x, ln_emb_g, ln_emb_b, w_in, lambda_q1, lambda_k1, lambda_q2, lambda_k2, subln_g, w_attn_o, w_fourier,
           w_gate, b_gate, w_mix_out, ln1_g, ln1_b, w_up, conv_w, conv_b, w_down, ln2_g, ln2_b):
    batch, seq, d = x.shape
    m = batch * seq
    depth = w_in.shape[0]
    assert depth == DEPTH
    attn_v = w_attn_o.shape[1]
    chans = w_fourier.shape[1]
    qk_cols = (w_in.shape[2] - attn_v - chans) // 2
    n_heads = attn_v // (2 * HEAD_DIM)
    assert qk_cols == attn_v and seq % SEQ_DFT_MAJOR == 0 and chans % FOURIER_GROUP == 0

    (h0, h0b) = _layer_norm(x.reshape(m, d), ln_emb_g, ln_emb_b, (F32, BF16))
    l = 0
    u = _in_proj(h0b, w_in[l].astype(BF16), seq, qk_cols)
    attn = _diff_attention(u, lambda_q1[l], lambda_k1[l], lambda_q2[l], lambda_k2[l], subln_g[l],
                           batch, seq, n_heads)
    four = _fourier_mix(u, batch, seq, 2 * qk_cols + attn_v, chans)
    mix = _gate_mix(h0b, attn, four, w_gate[l].astype(BF16), b_gate[l],
                    w_attn_o[l].astype(BF16), w_fourier[l].astype(BF16))
    s1 = _matmul_residual(mix, w_mix_out[l].astype(BF16), h0, TILES["mix_bm"], TILES["mix_bn"])
    (h1, h1b) = _layer_norm(s1, ln1_g[l], ln1_b[l], (F32, BF16))
    dff = w_down.shape[1]
    pad = -dff % FFN_PAD_MULTIPLE
    pad_halves = lambda a: jnp.pad(a.reshape(a.shape[0], 2, dff), ((0, 0), (0, 0), (0, pad))).reshape(a.shape[0], -1)
    w_up_b = pad_halves(w_up[l]).astype(BF16)
    w_down_b = jnp.pad(w_down[l], ((0, pad), (0, 0))).astype(BF16)
    act = _conv_ffn_up(h1b, w_up_b, pad_halves(conv_w[l]), pad_halves(conv_b[l].reshape(1, -1)), seq)
    s2 = _matmul_residual(act, w_down_b, h1, TILES["down_bm"], TILES["down_bn"])
    (out,) = _layer_norm(s2, ln2_g[l], ln2_b[l], (F32,))
    return out.reshape(batch, seq, d)
```

```python
import functools
import math

import numpy as np
import jax
import jax.numpy as jnp
from jax import lax
from jax.experimental import pallas as pl
from jax.experimental.pallas import tpu as pltpu

F32 = jnp.float32
BF16 = jnp.bfloat16

HEAD_DIM = 128
ROT_DIM = HEAD_DIM // 4
ROPE_THETA = 500000.0
FOURIER_GROUP = 256
CONV_WIDTH = 3
DEPTH = 1
ALPHA = (2.0 * DEPTH) ** 0.25
LN_EPS = 1e-5
LAMBDA_INIT = 0.8 - 0.6 * math.exp(-0.3 * 0)

LANES = 128
BF16_SUBLANES = 16
F32_SUBLANES = 8
V7X_VMEM_BYTES = 64 * 1024 * 1024
VMEM_CAP_BYTES = 60000 * 1024
SEQ_DFT_MAJOR = 128
HALO = BF16_SUBLANES

TILES = dict(ln_rows=256, in_bm=1024, in_bn=1024, in_chunk=256, in_row_chunk=128,
             attn_tq=2048, attn_rows=256, attn_tk=1024, chan_bm=2048, dft_a_jb=4,
             dft_b_cols=16384, gate_bm=512, gate_bn=512, mix_bm=1024, mix_bn=512, up_bm=1024, up_bn=512,
             up_chunk=256, up_row_chunk=128, down_bm=512, down_bn=256)
FFN_PAD_MULTIPLE = 1024


def _vmem_limit(pipelined_bytes, resident_bytes=0):
    need = 2 * pipelined_bytes + resident_bytes + (2 << 20)
    return int(min(max(need, 16 << 20), VMEM_CAP_BYTES))


def _blk(n, pref, align):
    if n <= pref:
        return n
    b = (pref // align) * align
    while b >= align:
        if n % b == 0:
            return b
        b -= align
    raise ValueError(f"no block of alignment {align} divides {n}")


def _params(sem, vmem):
    return pltpu.CompilerParams(dimension_semantics=sem, vmem_limit_bytes=vmem)


def _ln_math(x, g, b):
    mu = jnp.mean(x, axis=-1, keepdims=True)
    xc = x - mu
    var = jnp.mean(xc * xc, axis=-1, keepdims=True)
    return xc * lax.rsqrt(var + LN_EPS) * g + b


def _ln_body(x_ref, g_ref, b_ref, *o_refs):
    y = _ln_math(x_ref[...].astype(F32), g_ref[...], b_ref[...])
    for o_ref in o_refs:
        o_ref[...] = y.astype(o_ref.dtype)


def _layer_norm(x, g, b, out_dtypes):
    m, d = x.shape
    bm = _blk(m, TILES["ln_rows"], BF16_SUBLANES)
    row = pl.BlockSpec((bm, d), lambda i: (i, 0))
    vec = pl.BlockSpec((1, d), lambda i: (0, 0))
    blk_bytes = bm * d * (4 + sum(jnp.dtype(t).itemsize for t in out_dtypes))
    return pl.pallas_call(
        _ln_body,
        grid=(m // bm,),
        in_specs=[row, vec, vec],
        out_specs=[row] * len(out_dtypes),
        out_shape=[jax.ShapeDtypeStruct((m, d), t) for t in out_dtypes],
        compiler_params=_params(("parallel",), _vmem_limit(blk_bytes, 4 * bm * d * 4)),
        name="layer_norm",
    )(x, g.reshape(1, d).astype(F32), b.reshape(1, d).astype(F32))


def _in_proj_body(x_ref, w_ref, cos_ref, sin_ref, o_ref, *, chunk, row_chunk):
    x = x_ref[...]
    bm, bn = o_ref.shape
    lane = lax.broadcasted_iota(jnp.int32, (row_chunk, LANES), 1)
    first_half = lane < (ROT_DIM // 2)
    for ch in range(bn // chunk):
        acc = jnp.dot(x, w_ref[:, ch * chunk:(ch + 1) * chunk], preferred_element_type=F32)
        for r in range(bm // row_chunk):
            rws = slice(r * row_chunk, (r + 1) * row_chunk)
            c, s = cos_ref[rws, :], sin_ref[rws, :]
            for cb in range(chunk // LANES):
                t = acc[rws, cb * LANES:(cb + 1) * LANES]
                partner = jnp.where(first_half,
                                    pltpu.roll(t, LANES - ROT_DIM // 2, 1),
                                    pltpu.roll(t, ROT_DIM // 2, 1))
                col0 = ch * chunk + cb * LANES
                o_ref[rws, col0:col0 + LANES] = (t * c + partner * s).astype(o_ref.dtype)


def _rotary_tables(seq, q_scale):
    inv_freq = ROPE_THETA ** (-jnp.arange(0, ROT_DIM, 2, dtype=F32) / ROT_DIM)
    ang = jnp.arange(seq, dtype=F32)[:, None] * inv_freq[None, :]
    cos, sin = jnp.cos(ang), jnp.sin(ang)
    pad = HEAD_DIM - ROT_DIM
    cos_k = jnp.concatenate([cos, cos, jnp.ones((seq, pad), F32)], axis=1)
    sin_k = jnp.concatenate([-sin, sin, jnp.zeros((seq, pad), F32)], axis=1)
    cos_t = jnp.stack([cos_k * q_scale, cos_k, jnp.ones_like(cos_k)])
    sin_t = jnp.stack([sin_k * q_scale, sin_k, jnp.zeros_like(sin_k)])
    return cos_t, sin_t


def _in_proj(h0b, w_in_b, seq, qk_cols):
    m, d = h0b.shape
    n = w_in_b.shape[1]
    bm = _blk(seq, TILES["in_bm"], BF16_SUBLANES)
    bn = _blk(qk_cols, TILES["in_bn"], LANES)
    chunk = _blk(bn, TILES["in_chunk"], LANES)
    row_chunk = _blk(bm, TILES["in_row_chunk"], BF16_SUBLANES)
    cos_t, sin_t = _rotary_tables(seq, HEAD_DIM ** -0.5 * math.log2(math.e))
    n_q_blocks = qk_cols // bn
    tab = pl.BlockSpec((None, bm, LANES), lambda i, j: (jnp.minimum(j // n_q_blocks, 2), i % (seq // bm), 0))
    body = functools.partial(_in_proj_body, chunk=chunk, row_chunk=row_chunk)
    blk_bytes = bm * d * 2 + d * bn * 2 + bm * bn * 2 + 2 * bm * LANES * 4
    return pl.pallas_call(
        body,
        grid=(m // bm, n // bn),
        in_specs=[pl.BlockSpec((bm, d), lambda i, j: (i, 0)),
                  pl.BlockSpec((d, bn), lambda i, j: (0, j)),
                  tab, tab],
        out_specs=pl.BlockSpec((bm, bn), lambda i, j: (i, j)),
        out_shape=jax.ShapeDtypeStruct((m, n), BF16),
        compiler_params=_params(("parallel", "arbitrary"), _vmem_limit(blk_bytes, 12 * bm * chunk * 4)),
        name="in_proj",
    )(h0b, w_in_b, cos_t, sin_t)


def _attn_body(q_ref, k_ref, v_ref, lq1_ref, lk1_ref, lq2_ref, lk2_ref, g_ref, o_ref,
               s0_ref, s1_ref, p0_ref, p1_ref, a0_ref, a1_ref, m_ref, l_ref, acc_ref, *, rows, tk, nr, nt):
    s_refs, p_refs, a_refs = (s0_ref, s1_ref), (p0_ref, p1_ref), (a0_ref, a1_ref)
    n_items = nr * nt
    m_ref[...] = jnp.full(m_ref.shape, -jnp.inf, F32)
    l_ref[...] = jnp.zeros(l_ref.shape, F32)
    acc_ref[...] = jnp.zeros(acc_ref.shape, F32)

    def offsets(i):
        if isinstance(i, int):
            return (i % nr) * rows, (i // nr) * tk
        return pl.multiple_of(lax.rem(i, nr) * rows, rows), pl.multiple_of(lax.div(i, nr) * tk, tk)

    def qk(i, slot):
        row0, key0 = offsets(i)
        for c in range(2):
            cols = slice(c * HEAD_DIM, (c + 1) * HEAD_DIM)
            s_refs[slot][c] = lax.dot_general(q_ref[pl.ds(row0, rows), cols], k_ref[pl.ds(key0, tk), cols],
                                              (((1,), (1,)), ((), ())), preferred_element_type=F32)

    def softmax(i, slot):
        row0, _ = offsets(i)
        for c in range(2):
            s = s_refs[slot][c]
            m_old = m_ref[c, pl.ds(row0, rows), :]
            m_new = jnp.maximum(m_old, jnp.max(s, axis=1, keepdims=True))
            a = jnp.exp2(m_old - m_new)
            p = jnp.exp2(s - m_new)
            l_ref[c, pl.ds(row0, rows), :] = a * l_ref[c, pl.ds(row0, rows), :] + jnp.sum(p, axis=1, keepdims=True)
            m_ref[c, pl.ds(row0, rows), :] = m_new
            p_refs[slot][c] = p.astype(p_refs[slot].dtype)
            a_refs[slot][c] = a

    def pv(i, slot):
        row0, key0 = offsets(i)
        v = v_ref[pl.ds(key0, tk), :]
        for c in range(2):
            acc_ref[c, pl.ds(row0, rows), :] = (a_refs[slot][c] * acc_ref[c, pl.ds(row0, rows), :]
                                                + jnp.dot(p_refs[slot][c], v, preferred_element_type=F32))

    qk(0, 0)
    qk(1, 1)
    softmax(0, 0)

    def pair(j, carry):
        i0 = 2 * j
        pv(i0 - 2, 0)
        qk(i0, 0)
        softmax(i0 - 1, 1)
        pv(i0 - 1, 1)
        qk(i0 + 1, 1)
        softmax(i0, 0)
        return carry

    lax.fori_loop(1, n_items // 2, pair, 0)
    pv(n_items - 2, 0)
    softmax(n_items - 1, 1)
    pv(n_items - 1, 1)

    lam = (jnp.exp(jnp.sum(lq1_ref[...] * lk1_ref[...], axis=1, keepdims=True))
           - jnp.exp(jnp.sum(lq2_ref[...] * lk2_ref[...], axis=1, keepdims=True))
           + LAMBDA_INIT)
    o = acc_ref[0] / l_ref[0] - lam * (acc_ref[1] / l_ref[1])
    ms = jnp.mean(o * o, axis=-1, keepdims=True)
    o = o * lax.rsqrt(ms + LN_EPS) * g_ref[...] * (1.0 - LAMBDA_INIT)
    o_ref[...] = o.astype(o_ref.dtype)


def _diff_attention(u, lq1, lk1, lq2, lk2, subln_g, batch, seq, n_heads):
    m = u.shape[0]
    hw = 2 * HEAD_DIM
    tq = _blk(seq, TILES["attn_tq"], BF16_SUBLANES)
    rows = _blk(tq, TILES["attn_rows"], BF16_SUBLANES)
    tk = _blk(seq, TILES["attn_tk"], LANES)
    nq, nr, nt = seq // tq, tq // rows, seq // tk
    assert (nr * nt) % 2 == 0 and nr * nt >= 4 and nr >= 2
    vec = lambda a: a.reshape(1, -1).astype(F32)
    small = pl.BlockSpec((1, HEAD_DIM), lambda b, h, qi: (0, 0))
    blk_bytes = 2 * tq * hw * 2 + 2 * seq * hw * 2
    scratch_bytes = 4 * rows * tk * (4 + 2) + 4 * rows * LANES * 4 + 2 * tq * (hw + 2 * LANES) * 4
    return pl.pallas_call(
        functools.partial(_attn_body, rows=rows, tk=tk, nr=nr, nt=nt),
        grid=(batch, n_heads, nq),
        in_specs=[pl.BlockSpec((tq, hw), lambda b, h, qi: (b * nq + qi, h)),
                  pl.BlockSpec((seq, hw), lambda b, h, qi: (b, n_heads + h)),
                  pl.BlockSpec((seq, hw), lambda b, h, qi: (b, 2 * n_heads + h)),
                  small, small, small, small,
                  pl.BlockSpec((1, hw), lambda b, h, qi: (0, 0))],
        out_specs=pl.BlockSpec((tq, hw), lambda b, h, qi: (b * nq + qi, h)),
        out_shape=jax.ShapeDtypeStruct((m, n_heads * hw), BF16),
        scratch_shapes=[pltpu.VMEM((2, rows, tk), F32), pltpu.VMEM((2, rows, tk), F32),
                        pltpu.VMEM((2, rows, tk), BF16), pltpu.VMEM((2, rows, tk), BF16),
                        pltpu.VMEM((2, rows, 1), F32), pltpu.VMEM((2, rows, 1), F32),
                        pltpu.VMEM((2, tq, 1), F32), pltpu.VMEM((2, tq, 1), F32),
                        pltpu.VMEM((2, tq, hw), F32)],
        compiler_params=_params(("parallel", "parallel", "arbitrary"),
                                _vmem_limit(blk_bytes, scratch_bytes + 2 * rows * tk * 4 + 6 * tq * hw * 4)),
        name="diff_attn",
    )(u, u, u, vec(lq1), vec(lk1), vec(lq2), vec(lk2), vec(subln_g))


def _dft_consts(seq):
    def cs(n, rows, cols):
        ph = 2.0 * np.pi * ((np.arange(rows)[:, None] * np.arange(cols)[None, :]) % n) / n
        return np.cos(ph), np.sin(ph)

    g = FOURIER_GROUP
    cc, sc = cs(g, g, g)
    w_chan = np.concatenate([cc, -sc], axis=1) / math.sqrt(g)
    na = SEQ_DFT_MAJOR
    nb = seq // na
    ca, sa = cs(na, na, na)
    f_a = np.block([[ca, sa], [-sa, ca]])
    tc, ts = cs(seq, nb, na)
    tw_c = (tc / math.sqrt(seq))[:, :, None]
    tw_s = (ts / math.sqrt(seq))[:, :, None]
    cb, sb = cs(nb, nb, nb)
    f_b = np.concatenate([cb, sb], axis=1)
    as32 = lambda a: jnp.asarray(a.astype(np.float32))
    return as32(w_chan), as32(f_a), as32(tw_c), as32(tw_s), as32(f_b)


def _chan_dft_body(x_ref, w_ref, z_ref):
    r = jnp.dot(x_ref[...], w_ref[...], preferred_element_type=F32)
    g = x_ref.shape[1]
    z_ref[0] = r[:, :g].astype(z_ref.dtype)
    z_ref[1] = r[:, g:].astype(z_ref.dtype)


def _dft_a_body(z_ref, fa_ref, twc_ref, tws_ref, t_ref, *, jb, chans, na):
    fa = fa_ref[...]
    for jj in range(jb):
        cols = slice(jj * chans, (jj + 1) * chans)
        zz = jnp.concatenate([z_ref[0, :, cols], z_ref[1, :, cols]], axis=0)
        r = jnp.dot(fa, zz, preferred_element_type=F32)
        tr, ti = r[:na], r[na:]
        c, s = twc_ref[jj], tws_ref[jj]
        t_ref[0, jj] = (tr * c + ti * s).astype(t_ref.dtype)
        t_ref[1, jj] = (ti * c - tr * s).astype(t_ref.dtype)


def _dft_b_body(t_ref, fb_ref, y_ref):
    y_ref[...] = jnp.dot(fb_ref[...], t_ref[...], preferred_element_type=F32).astype(y_ref.dtype)


def _fourier_mix(u, batch, seq, f_col0, chans):
    m = u.shape[0]
    g = FOURIER_GROUP
    n_groups = chans // g
    w_chan, f_a, tw_c, tw_s, f_b = _dft_consts(seq)
    na, nb = SEQ_DFT_MAJOR, seq // SEQ_DFT_MAJOR

    bm = _blk(m, TILES["chan_bm"], BF16_SUBLANES)
    z = pl.pallas_call(
        _chan_dft_body,
        grid=(m // bm, n_groups),
        in_specs=[pl.BlockSpec((bm, g), lambda i, gi: (i, f_col0 // g + gi)),
                  pl.BlockSpec((g, 2 * g), lambda i, gi: (0, 0))],
        out_specs=pl.BlockSpec((2, bm, g), lambda i, gi: (0, i, gi)),
        out_shape=jax.ShapeDtypeStruct((2, m, chans), BF16),
        compiler_params=_params(("parallel", "parallel"), _vmem_limit(3 * bm * g * 2, 2 * bm * g * 4)),
        name="chan_dft",
    )(u, w_chan.astype(BF16))

    jb = _blk(nb, TILES["dft_a_jb"], 1)
    z4 = z.reshape(2, batch, na, nb * chans)
    t = pl.pallas_call(
        functools.partial(_dft_a_body, jb=jb, chans=chans, na=na),
        grid=(batch, nb // jb),
        in_specs=[pl.BlockSpec((2, None, na, jb * chans), lambda b, j: (0, b, 0, j)),
                  pl.BlockSpec((2 * na, 2 * na), lambda b, j: (0, 0)),
                  pl.BlockSpec((jb, na, 1), lambda b, j: (j, 0, 0)),
                  pl.BlockSpec((jb, na, 1), lambda b, j: (j, 0, 0))],
        out_specs=pl.BlockSpec((None, 2, jb, na, chans), lambda b, j: (b, 0, j, 0, 0)),
        out_shape=jax.ShapeDtypeStruct((batch, 2, nb, na, chans), BF16),
        compiler_params=_params(("parallel", "parallel"),
                                _vmem_limit(4 * na * jb * chans * 2 + 2 * jb * na * LANES * 4,
                                            6 * na * chans * 4)),
        name="seq_dft_a",
    )(z4, f_a.astype(BF16), tw_c, tw_s)

    t2 = t.reshape(batch, 2 * nb, na * chans)
    bc = _blk(na * chans, TILES["dft_b_cols"], LANES)
    y = pl.pallas_call(
        _dft_b_body,
        grid=(batch, na * chans // bc),
        in_specs=[pl.BlockSpec((None, 2 * nb, bc), lambda b, j: (b, 0, j)),
                  pl.BlockSpec((nb, 2 * nb), lambda b, j: (0, 0))],
        out_specs=pl.BlockSpec((None, nb, bc), lambda b, j: (b, 0, j)),
        out_shape=jax.ShapeDtypeStruct((batch, nb, na * chans), BF16),
        compiler_params=_params(("parallel", "parallel"), _vmem_limit(3 * nb * bc * 2, nb * bc * 4)),
        name="seq_dft_b",
    )(t2, f_b.astype(BF16))
    return y.reshape(m, chans)


def _sigmoid(x):
    return 1.0 / (1.0 + jnp.exp(-x))


def _gate_mix_body(h_ref, a_ref, f_ref, wga_ref, wgf_ref, bga_ref, bgf_ref, wao_ref, wfo_ref, o_ref):
    h = h_ref[...]
    ga = _sigmoid(jnp.dot(h, wga_ref[...], preferred_element_type=F32) + bga_ref[...])
    gf = _sigmoid(jnp.dot(h, wgf_ref[...], preferred_element_type=F32) + bgf_ref[...])
    ya = jnp.dot(a_ref[...], wao_ref[...], preferred_element_type=F32)
    yf = jnp.dot(f_ref[...], wfo_ref[...], preferred_element_type=F32)
    o_ref[...] = (ga * ya + gf * yf).astype(o_ref.dtype)


def _gate_mix(h0b, attn, four, w_gate_b, b_gate, w_ao_b, w_fo_b):
    m, d = h0b.shape
    ka, kf = attn.shape[1], four.shape[1]
    bm = _blk(m, TILES["gate_bm"], BF16_SUBLANES)
    bn = _blk(d, TILES["gate_bn"], LANES)
    nb = d // bn
    bg = b_gate.reshape(1, 2 * d).astype(F32)
    blk_bytes = (bm * (d + ka + kf) + (2 * d + ka + kf) * bn + bm * bn) * 2
    return pl.pallas_call(
        _gate_mix_body,
        grid=(m // bm, nb),
        in_specs=[pl.BlockSpec((bm, d), lambda i, j: (i, 0)),
                  pl.BlockSpec((bm, ka), lambda i, j: (i, 0)),
                  pl.BlockSpec((bm, kf), lambda i, j: (i, 0)),
                  pl.BlockSpec((d, bn), lambda i, j: (0, j)),
                  pl.BlockSpec((d, bn), lambda i, j: (0, nb + j)),
                  pl.BlockSpec((1, bn), lambda i, j: (0, j)),
                  pl.BlockSpec((1, bn), lambda i, j: (0, nb + j)),
                  pl.BlockSpec((ka, bn), lambda i, j: (0, j)),
                  pl.BlockSpec((kf, bn), lambda i, j: (0, j))],
        out_specs=pl.BlockSpec((bm, bn), lambda i, j: (i, j)),
        out_shape=jax.ShapeDtypeStruct((m, d), BF16),
        compiler_params=_params(("parallel", "arbitrary"), _vmem_limit(blk_bytes, 6 * bm * bn * 4)),
        name="gate_mix",
    )(h0b, attn, four, w_gate_b, w_gate_b, bg, bg, w_ao_b, w_fo_b)


def _mm_res_body(x_ref, w_ref, r_ref, o_ref):
    o_ref[...] = ALPHA * r_ref[...] + jnp.dot(x_ref[...], w_ref[...], preferred_element_type=F32)


def _matmul_residual(x, w, res, bm_pref, bn_pref):
    m, k = x.shape
    n = w.shape[1]
    bm = _blk(m, bm_pref, BF16_SUBLANES)
    bn = _blk(n, bn_pref, LANES)
    blk_bytes = (bm * k + k * bn) * 2 + 2 * bm * bn * 4
    return pl.pallas_call(
        _mm_res_body,
        grid=(m // bm, n // bn),
        in_specs=[pl.BlockSpec((bm, k), lambda i, j: (i, 0)),
                  pl.BlockSpec((k, bn), lambda i, j: (0, j)),
                  pl.BlockSpec((bm, bn), lambda i, j: (i, j))],
        out_specs=pl.BlockSpec((bm, bn), lambda i, j: (i, j)),
        out_shape=jax.ShapeDtypeStruct((m, n), F32),
        compiler_params=_params(("parallel", "arbitrary"), _vmem_limit(blk_bytes, bm * bn * 4)),
        name="matmul_residual",
    )(x, w, res)


def _up_body(x_ref, xp_ref, xn_ref, wg_ref, wv_ref, cwg_ref, cwv_ref, cbg_ref, cbv_ref, o_ref, xs_ref, a_ref,
             *, bm, blocks_per_seq, chunk, row_chunk, n_col_blocks):
    i = pl.program_id(0)
    j = pl.program_id(1)
    rows = bm + 2 * HALO
    n_chunks = o_ref.shape[1] // chunk

    @pl.when(j == 0)
    def _():
        pos = i % blocks_per_seq
        xs_ref[0:HALO, :] = jnp.where(pos != 0, xp_ref[...], jnp.zeros_like(xp_ref))
        xs_ref[HALO:HALO + bm, :] = x_ref[...]
        xs_ref[HALO + bm:rows, :] = jnp.where(pos != blocks_per_seq - 1, xn_ref[...], jnp.zeros_like(xn_ref))
        a_ref[1] = jnp.zeros(a_ref.shape[1:], F32)

    win = row_chunk + 2 * F32_SUBLANES

    def conv(a, cw, cb):
        inner = slice(F32_SUBLANES, F32_SUBLANES + row_chunk)
        prev = pltpu.roll(a, 1, 0)[inner]
        nxt = pltpu.roll(a, win - 1, 0)[inner]
        return prev * cw[0:1] + a[inner] * cw[1:2] + nxt * cw[2:3] + cb

    def epilogue(slot):
        for ch in range(n_chunks):
            cols = slice(ch * chunk, (ch + 1) * chunk)
            cwg, cwv, cbg, cbv = cwg_ref[:, cols], cwv_ref[:, cols], cbg_ref[:, cols], cbv_ref[:, cols]
            for r in range(bm // row_chunk):
                src = pl.ds(HALO - F32_SUBLANES + r * row_chunk, win)
                gate = conv(a_ref[slot, 2 * ch, src, :], cwg, cbg)
                val = conv(a_ref[slot, 2 * ch + 1, src, :], cwv, cbv)
                o_ref[r * row_chunk:(r + 1) * row_chunk, cols] = (gate * _sigmoid(gate) * val).astype(o_ref.dtype)

    def matmuls(slot):
        xs = xs_ref[...]
        for ch in range(n_chunks):
            cols = slice(ch * chunk, (ch + 1) * chunk)
            a_ref[slot, 2 * ch] = jnp.dot(xs, wg_ref[:, cols], preferred_element_type=F32)
            a_ref[slot, 2 * ch + 1] = jnp.dot(xs, wv_ref[:, cols], preferred_element_type=F32)

    for parity in range(2):
        @pl.when(jnp.logical_and(j % 2 == parity, j < n_col_blocks))
        def _():
            epilogue(1 - parity)
            matmuls(parity)

    @pl.when(j == n_col_blocks)
    def _():
        epilogue((n_col_blocks - 1) % 2)


def _conv_ffn_up(h1b, w_up_b, conv_w, conv_b, seq):
    m, d = h1b.shape
    dff = w_up_b.shape[1] // 2
    bm = _blk(seq, TILES["up_bm"], HALO)
    bn = _blk(dff, TILES["up_bn"], LANES)
    nb = dff // bn
    hb = bm // HALO
    last = m // HALO - 1
    rows = bm + 2 * HALO
    cw = conv_w.astype(F32)
    cb = conv_b.reshape(1, 2 * dff).astype(F32)
    chunk = _blk(bn, TILES["up_chunk"], LANES)
    row_chunk = _blk(bm, TILES["up_row_chunk"], BF16_SUBLANES)
    body = functools.partial(_up_body, bm=bm, blocks_per_seq=seq // bm, chunk=chunk, row_chunk=row_chunk,
                             n_col_blocks=nb)
    wcol = lambda j: jnp.minimum(j, nb - 1)
    ecol = lambda j: jnp.maximum(j - 1, 0)
    blk_bytes = 2 * d * bn * 2 + bm * bn * 2 + 8 * bn * 4 + 2 * HALO * d * 2
    resident_bytes = bm * d * 2 + rows * d * 2 + 4 * bn * rows * 4 + 8 * rows * chunk * 4
    return pl.pallas_call(
        body,
        grid=(m // bm, nb + 1),
        in_specs=[pl.BlockSpec((bm, d), lambda i, j: (i, 0), pipeline_mode=pl.Buffered(1)),
                  pl.BlockSpec((HALO, d), lambda i, j: (jnp.maximum(i * hb - 1, 0), 0)),
                  pl.BlockSpec((HALO, d), lambda i, j: (jnp.minimum((i + 1) * hb, last), 0)),
                  pl.BlockSpec((d, bn), lambda i, j: (0, wcol(j))),
                  pl.BlockSpec((d, bn), lambda i, j: (0, nb + wcol(j))),
                  pl.BlockSpec((CONV_WIDTH, bn), lambda i, j: (0, ecol(j))),
                  pl.BlockSpec((CONV_WIDTH, bn), lambda i, j: (0, nb + ecol(j))),
                  pl.BlockSpec((1, bn), lambda i, j: (0, ecol(j))),
                  pl.BlockSpec((1, bn), lambda i, j: (0, nb + ecol(j)))],
        out_specs=pl.BlockSpec((bm, bn), lambda i, j: (i, ecol(j))),
        out_shape=jax.ShapeDtypeStruct((m, dff), BF16),
        scratch_shapes=[pltpu.VMEM((rows, d), BF16),
                        pltpu.VMEM((2, 2 * bn // chunk, rows, chunk), F32)],
        compiler_params=_params(("parallel", "arbitrary"), _vmem_limit(blk_bytes, resident_bytes)),
        name="conv_ffn_up",
    )(h1b, h1b, h1b, w_up_b, w_up_b, cw, cw, cb, cb)


def kernel(x, ln_emb_g, ln_emb_b, w_in, lambda_q1, lambda_k1, lambda_q2, lambda_k2, subln_g, w_attn_o, w_fourier,
           w_gate, b_gate, w_mix_out, ln1_g, ln1_b, w_up, conv_w, conv_b, w_down, ln2_g, ln2_b):
    batch, seq, d = x.shape
    m = batch * seq
    depth = w_in.shape[0]
    assert depth == DEPTH
    attn_v = w_attn_o.shape[1]
    chans = w_fourier.shape[1]
    qk_cols = (w_in.shape[2] - attn_v - chans) // 2
    n_heads = attn_v // (2 * HEAD_DIM)
    assert qk_cols == attn_v and seq % SEQ_DFT_MAJOR == 0 and chans % FOURIER_GROUP == 0

    (h0, h0b) = _layer_norm(x.reshape(m, d), ln_emb_g, ln_emb_b, (F32, BF16))
    l = 0
    u = _in_proj(h0b, w_in[l].astype(BF16), seq, qk_cols)
    attn = _diff_attention(u, lambda_q1[l], lambda_k1[l], lambda_q2[l], lambda_k2[l], subln_g[l],
                           batch, seq, n_heads)
    four = _fourier_mix(u, batch, seq, 2 * qk_cols + attn_v, chans)
    mix = _gate_mix(h0b, attn, four, w_gate[l].astype(BF16), b_gate[l],
                    w_attn_o[l].astype(BF16), w_fourier[l].astype(BF16))
    s1 = _matmul_residual(mix, w_mix_out[l].astype(BF16), h0, TILES["mix_bm"], TILES["mix_bn"])
    (h1, h1b) = _layer_norm(s1, ln1_g[l], ln1_b[l], (F32, BF16))
    dff = w_down.shape[1]
    pad = -dff % FFN_PAD_MULTIPLE
    pad_cols = lambda a: jnp.pad(a, ((0, 0), (0, pad)))
    pad_halves = lambda a: jnp.concatenate([pad_cols(a[:, :dff]), pad_cols(a[:, dff:])], axis=1)
    w_up_b = pad_halves(w_up[l].astype(BF16))
    w_down_b = jnp.pad(w_down[l], ((0, pad), (0, 0))).astype(BF16)
    act = _conv_ffn_up(h1b, w_up_b, pad_halves(conv_w[l]), pad_halves(conv_b[l].reshape(1, -1)), seq)
    s2 = _matmul_residual(act, w_down_b, h1, TILES["down_bm"], TILES["down_bn"])
    (out,) = _layer_norm(s2, ln2_g[l], ln2_b[l], (F32,))
    return out.reshape(batch, seq, d)
```

```python
import functools
import math

import numpy as np
import jax
import jax.numpy as jnp
from jax import lax
from jax.experimental import pallas as pl
from jax.experimental.pallas import tpu as pltpu

F32 = jnp.float32
BF16 = jnp.bfloat16

HEAD_DIM = 128
ROT_DIM = HEAD_DIM // 4
ROPE_THETA = 500000.0
FOURIER_GROUP = 256
CONV_WIDTH = 3
DEPTH = 1
ALPHA = (2.0 * DEPTH) ** 0.25
LN_EPS = 1e-5
LAMBDA_INIT = 0.8 - 0.6 * math.exp(-0.3 * 0)

LANES = 128
BF16_SUBLANES = 16
F32_SUBLANES = 8
V7X_VMEM_BYTES = 64 * 1024 * 1024
VMEM_CAP_BYTES = 60000 * 1024
SEQ_DFT_MAJOR = 128
HALO = BF16_SUBLANES
SWAP = BF16_SUBLANES

TILES = dict(ln_rows=256, in_bm=1024, in_bn=1024, in_chunk=256, in_row_chunk=128,
             attn_tq=2048, attn_rows=256, attn_tk=1024, dft_a_jb=4, dft_b_cols=1024,
             gate_bm=512, gate_bn=512, mix_bm=1024, mix_bn=512, up_bm=1024, up_bn=512, up_chunk=256,
             down_bm=512, down_bn=256)
FFN_PAD_MULTIPLE = 1024


def _vmem_limit(pipelined_bytes, resident_bytes=0):
    need = 2 * pipelined_bytes + resident_bytes + (2 << 20)
    return int(min(max(need, 16 << 20), VMEM_CAP_BYTES))


def _blk(n, pref, align):
    if n <= pref:
        return n
    b = (pref // align) * align
    while b >= align:
        if n % b == 0:
            return b
        b -= align
    raise ValueError(f"no block of alignment {align} divides {n}")


def _params(sem, vmem):
    return pltpu.CompilerParams(dimension_semantics=sem, vmem_limit_bytes=vmem)


def _ln_math(x, g, b):
    mu = jnp.mean(x, axis=-1, keepdims=True)
    xc = x - mu
    var = jnp.mean(xc * xc, axis=-1, keepdims=True)
    return xc * lax.rsqrt(var + LN_EPS) * g + b


def _ln_body(x_ref, g_ref, b_ref, *o_refs):
    y = _ln_math(x_ref[...].astype(F32), g_ref[...], b_ref[...])
    for o_ref in o_refs:
        o_ref[...] = y.astype(o_ref.dtype)


def _layer_norm(x, g, b, out_dtypes):
    m, d = x.shape
    bm = _blk(m, TILES["ln_rows"], BF16_SUBLANES)
    row = pl.BlockSpec((bm, d), lambda i: (i, 0))
    vec = pl.BlockSpec((1, d), lambda i: (0, 0))
    blk_bytes = bm * d * (4 + sum(jnp.dtype(t).itemsize for t in out_dtypes))
    return pl.pallas_call(
        _ln_body,
        grid=(m // bm,),
        in_specs=[row, vec, vec],
        out_specs=[row] * len(out_dtypes),
        out_shape=[jax.ShapeDtypeStruct((m, d), t) for t in out_dtypes],
        compiler_params=_params(("parallel",), _vmem_limit(blk_bytes, 4 * bm * d * 4)),
        name="layer_norm",
    )(x, g.reshape(1, d).astype(F32), b.reshape(1, d).astype(F32))


def _in_proj_body(x_ref, w_ref, cos_ref, sin_ref, o_ref, *, chunk, row_chunk):
    x = x_ref[...]
    bm, bn = o_ref.shape
    lane = lax.broadcasted_iota(jnp.int32, (row_chunk, LANES), 1)
    first_half = lane < (ROT_DIM // 2)
    for ch in range(bn // chunk):
        acc = jnp.dot(x, w_ref[:, ch * chunk:(ch + 1) * chunk], preferred_element_type=F32)
        for r in range(bm // row_chunk):
            rws = slice(r * row_chunk, (r + 1) * row_chunk)
            c, s = cos_ref[rws, :], sin_ref[rws, :]
            for cb in range(chunk // LANES):
                t = acc[rws, cb * LANES:(cb + 1) * LANES]
                partner = jnp.where(first_half,
                                    pltpu.roll(t, LANES - ROT_DIM // 2, 1),
                                    pltpu.roll(t, ROT_DIM // 2, 1))
                col0 = ch * chunk + cb * LANES
                o_ref[rws, col0:col0 + LANES] = (t * c + partner * s).astype(o_ref.dtype)


def _rotary_tables(seq, q_scale):
    inv_freq = ROPE_THETA ** (-jnp.arange(0, ROT_DIM, 2, dtype=F32) / ROT_DIM)
    ang = jnp.arange(seq, dtype=F32)[:, None] * inv_freq[None, :]
    cos, sin = jnp.cos(ang), jnp.sin(ang)
    pad = HEAD_DIM - ROT_DIM
    cos_k = jnp.concatenate([cos, cos, jnp.ones((seq, pad), F32)], axis=1)
    sin_k = jnp.concatenate([-sin, sin, jnp.zeros((seq, pad), F32)], axis=1)
    cos_t = jnp.stack([cos_k * q_scale, cos_k, jnp.ones_like(cos_k)])
    sin_t = jnp.stack([sin_k * q_scale, sin_k, jnp.zeros_like(sin_k)])
    return cos_t, sin_t


def _in_proj(h0b, w_in_b, seq, qk_cols):
    m, d = h0b.shape
    n = w_in_b.shape[1]
    bm = _blk(seq, TILES["in_bm"], BF16_SUBLANES)
    bn = _blk(qk_cols, TILES["in_bn"], LANES)
    chunk = _blk(bn, TILES["in_chunk"], LANES)
    row_chunk = _blk(bm, TILES["in_row_chunk"], BF16_SUBLANES)
    cos_t, sin_t = _rotary_tables(seq, HEAD_DIM ** -0.5 * math.log2(math.e))
    n_q_blocks = qk_cols // bn
    tab = pl.BlockSpec((None, bm, LANES), lambda i, j: (jnp.minimum(j // n_q_blocks, 2), i % (seq // bm), 0))
    body = functools.partial(_in_proj_body, chunk=chunk, row_chunk=row_chunk)
    blk_bytes = bm * d * 2 + d * bn * 2 + bm * bn * 2 + 2 * bm * LANES * 4
    return pl.pallas_call(
        body,
        grid=(m // bm, n // bn),
        in_specs=[pl.BlockSpec((bm, d), lambda i, j: (i, 0)),
                  pl.BlockSpec((d, bn), lambda i, j: (0, j)),
                  tab, tab],
        out_specs=pl.BlockSpec((bm, bn), lambda i, j: (i, j)),
        out_shape=jax.ShapeDtypeStruct((m, n), BF16),
        compiler_params=_params(("parallel", "arbitrary"), _vmem_limit(blk_bytes, 12 * bm * chunk * 4)),
        name="in_proj",
    )(h0b, w_in_b, cos_t, sin_t)


def _attn_body(q_ref, k_ref, v_ref, lq1_ref, lk1_ref, lq2_ref, lk2_ref, g_ref, o_ref,
               s0_ref, s1_ref, p0_ref, p1_ref, a0_ref, a1_ref, m_ref, l_ref, acc_ref, *, rows, tk, nr, nt):
    s_refs, p_refs, a_refs = (s0_ref, s1_ref), (p0_ref, p1_ref), (a0_ref, a1_ref)
    n_items = nr * nt
    m_ref[...] = jnp.full(m_ref.shape, -jnp.inf, F32)
    l_ref[...] = jnp.zeros(l_ref.shape, F32)
    acc_ref[...] = jnp.zeros(acc_ref.shape, F32)

    def offsets(i):
        if isinstance(i, int):
            return (i % nr) * rows, (i // nr) * tk
        return pl.multiple_of(lax.rem(i, nr) * rows, rows), pl.multiple_of(lax.div(i, nr) * tk, tk)

    def qk(i, slot):
        row0, key0 = offsets(i)
        for c in range(2):
            cols = slice(c * HEAD_DIM, (c + 1) * HEAD_DIM)
            s_refs[slot][c] = lax.dot_general(q_ref[pl.ds(row0, rows), cols], k_ref[pl.ds(key0, tk), cols],
                                              (((1,), (1,)), ((), ())), preferred_element_type=F32)

    def softmax(i, slot):
        row0, _ = offsets(i)
        for c in range(2):
            s = s_refs[slot][c]
            m_old = m_ref[c, pl.ds(row0, rows), :]
            m_new = jnp.maximum(m_old, jnp.max(s, axis=1, keepdims=True))
            a = jnp.exp2(m_old - m_new)
            p = jnp.exp2(s - m_new)
            l_ref[c, pl.ds(row0, rows), :] = a * l_ref[c, pl.ds(row0, rows), :] + jnp.sum(p, axis=1, keepdims=True)
            m_ref[c, pl.ds(row0, rows), :] = m_new
            p_refs[slot][c] = p.astype(p_refs[slot].dtype)
            a_refs[slot][c] = a

    def pv(i, slot):
        row0, key0 = offsets(i)
        v = v_ref[pl.ds(key0, tk), :]
        for c in range(2):
            acc_ref[c, pl.ds(row0, rows), :] = (a_refs[slot][c] * acc_ref[c, pl.ds(row0, rows), :]
                                                + jnp.dot(p_refs[slot][c], v, preferred_element_type=F32))

    qk(0, 0)
    qk(1, 1)
    softmax(0, 0)

    def pair(j, carry):
        i0 = 2 * j
        pv(i0 - 2, 0)
        qk(i0, 0)
        softmax(i0 - 1, 1)
        pv(i0 - 1, 1)
        qk(i0 + 1, 1)
        softmax(i0, 0)
        return carry

    lax.fori_loop(1, n_items // 2, pair, 0)
    pv(n_items - 2, 0)
    softmax(n_items - 1, 1)
    pv(n_items - 1, 1)

    lam = (jnp.exp(jnp.sum(lq1_ref[...] * lk1_ref[...], axis=1, keepdims=True))
           - jnp.exp(jnp.sum(lq2_ref[...] * lk2_ref[...], axis=1, keepdims=True))
           + LAMBDA_INIT)
    o = acc_ref[0] / l_ref[0] - lam * (acc_ref[1] / l_ref[1])
    ms = jnp.mean(o * o, axis=-1, keepdims=True)
    o = o * lax.rsqrt(ms + LN_EPS) * g_ref[...] * (1.0 - LAMBDA_INIT)
    o_ref[...] = o.astype(o_ref.dtype)


def _diff_attention(u, lq1, lk1, lq2, lk2, subln_g, batch, seq, n_heads):
    m = u.shape[0]
    hw = 2 * HEAD_DIM
    tq = _blk(seq, TILES["attn_tq"], BF16_SUBLANES)
    rows = _blk(tq, TILES["attn_rows"], BF16_SUBLANES)
    tk = _blk(seq, TILES["attn_tk"], LANES)
    nq, nr, nt = seq // tq, tq // rows, seq // tk
    assert (nr * nt) % 2 == 0 and nr * nt >= 4 and nr >= 2
    vec = lambda a: a.reshape(1, -1).astype(F32)
    small = pl.BlockSpec((1, HEAD_DIM), lambda b, h, qi: (0, 0))
    blk_bytes = 2 * tq * hw * 2 + 2 * seq * hw * 2
    scratch_bytes = 4 * rows * tk * (4 + 2) + 4 * rows * LANES * 4 + 2 * tq * (hw + 2 * LANES) * 4
    return pl.pallas_call(
        functools.partial(_attn_body, rows=rows, tk=tk, nr=nr, nt=nt),
        grid=(batch, n_heads, nq),
        in_specs=[pl.BlockSpec((tq, hw), lambda b, h, qi: (b * nq + qi, h)),
                  pl.BlockSpec((seq, hw), lambda b, h, qi: (b, n_heads + h)),
                  pl.BlockSpec((seq, hw), lambda b, h, qi: (b, 2 * n_heads + h)),
                  small, small, small, small,
                  pl.BlockSpec((1, hw), lambda b, h, qi: (0, 0))],
        out_specs=pl.BlockSpec((tq, hw), lambda b, h, qi: (b * nq + qi, h)),
        out_shape=jax.ShapeDtypeStruct((m, n_heads * hw), BF16),
        scratch_shapes=[pltpu.VMEM((2, rows, tk), F32), pltpu.VMEM((2, rows, tk), F32),
                        pltpu.VMEM((2, rows, tk), BF16), pltpu.VMEM((2, rows, tk), BF16),
                        pltpu.VMEM((2, rows, 1), F32), pltpu.VMEM((2, rows, 1), F32),
                        pltpu.VMEM((2, tq, 1), F32), pltpu.VMEM((2, tq, 1), F32),
                        pltpu.VMEM((2, tq, hw), F32)],
        compiler_params=_params(("parallel", "parallel", "arbitrary"),
                                _vmem_limit(blk_bytes, scratch_bytes + 2 * rows * tk * 4 + 6 * tq * hw * 4)),
        name="diff_attn",
    )(u, u, u, vec(lq1), vec(lk1), vec(lq2), vec(lk2), vec(subln_g))


def _dft_consts(seq):
    def cs(n, rows, cols):
        ph = 2.0 * np.pi * ((np.arange(rows)[:, None] * np.arange(cols)[None, :]) % n) / n
        return np.cos(ph), np.sin(ph)

    g = FOURIER_GROUP
    cc, sc = cs(g, g, g)
    w_chan = np.concatenate([cc, -sc], axis=1) / math.sqrt(g)
    na = SEQ_DFT_MAJOR
    nb = seq // na
    ca, sa = cs(na, na, na)
    f_a = np.block([[ca, sa], [-sa, ca]])
    tc, ts = cs(seq, nb, na)
    tw_c = (tc / math.sqrt(seq))[:, :, None]
    tw_s = (ts / math.sqrt(seq))[:, :, None]
    cb, sb = cs(nb, nb, nb)
    f_b = np.concatenate([cb, sb], axis=1)
    r = np.arange(SWAP * SWAP)
    swap = np.zeros((SWAP * SWAP, SWAP * SWAP))
    swap[r, (r % SWAP) * SWAP + r // SWAP] = 1.0
    as32 = lambda a: jnp.asarray(a.astype(np.float32))
    return as32(w_chan), as32(f_a), as32(tw_c), as32(tw_s), as32(f_b), as32(swap)


def _tiles(ref_or_val, starts):
    return jnp.concatenate([ref_or_val[s:s + SWAP] for s in starts], axis=0)


def _chan_dft_body(x_ref, swap_ref, w_ref, z_ref, *, nb):
    g = x_ref.shape[1]
    swap = swap_ref[...]
    w = w_ref[...]
    for n1 in range(nb // SWAP):
        xin = _tiles(x_ref, [a * nb + n1 * SWAP for a in range(SWAP)])
        xp = jnp.dot(swap, xin, preferred_element_type=F32).astype(x_ref.dtype)
        r = jnp.dot(xp, w, preferred_element_type=F32)
        for n2 in range(SWAP):
            rows = slice(n2 * SWAP, (n2 + 1) * SWAP)
            z_ref[0, n1 * SWAP + n2] = r[rows, :g].astype(z_ref.dtype)
            z_ref[1, n1 * SWAP + n2] = r[rows, g:].astype(z_ref.dtype)


def _dft_a_body(z_ref, fa_ref, twc_ref, tws_ref, t_ref, *, jb, na):
    fa = fa_ref[...]
    for jj in range(jb):
        zz = jnp.concatenate([z_ref[0, jj], z_ref[1, jj]], axis=0)
        r = jnp.dot(fa, zz, preferred_element_type=F32)
        tr, ti = r[:na], r[na:]
        c, s = twc_ref[jj], tws_ref[jj]
        t_ref[0, jj] = (tr * c + ti * s).astype(t_ref.dtype)
        t_ref[1, jj] = (ti * c - tr * s).astype(t_ref.dtype)


def _dft_b_body(t_ref, swap_ref, fb_ref, y_ref, *, nb):
    swap = swap_ref[...]
    fb = fb_ref[...]
    dt = t_ref.dtype
    q = []
    for p1 in range(2 * nb // SWAP):
        part, nb0 = divmod(p1 * SWAP, nb)
        xin = jnp.concatenate([t_ref[part, nb0 + p2] for p2 in range(SWAP)], axis=0)
        q.append(jnp.dot(swap, xin, preferred_element_type=F32).astype(dt))
    ys = []
    for ka in range(SWAP):
        m_ka = jnp.concatenate([qq[ka * SWAP:(ka + 1) * SWAP] for qq in q], axis=0)
        ys.append(jnp.dot(fb, m_ka, preferred_element_type=F32).astype(dt))
    for k1 in range(nb // SWAP):
        vin = jnp.concatenate([yy[k1 * SWAP:(k1 + 1) * SWAP] for yy in ys], axis=0)
        out = jnp.dot(swap, vin, preferred_element_type=F32).astype(y_ref.dtype)
        for k2 in range(SWAP):
            y_ref[k1 * SWAP + k2] = out[k2 * SWAP:(k2 + 1) * SWAP]


def _fourier_mix(u, batch, seq, f_col0, chans):
    m = u.shape[0]
    g = FOURIER_GROUP
    n_groups = chans // g
    w_chan, f_a, tw_c, tw_s, f_b, swap = _dft_consts(seq)
    na, nb = SEQ_DFT_MAJOR, seq // SEQ_DFT_MAJOR
    assert na % SWAP == 0 and nb % SWAP == 0
    swap = swap.astype(BF16)

    rows = SWAP * nb
    steps_per_batch = seq // rows
    z = pl.pallas_call(
        functools.partial(_chan_dft_body, nb=nb),
        grid=(m // rows, n_groups),
        in_specs=[pl.BlockSpec((rows, g), lambda i, gi: (i, f_col0 // g + gi)),
                  pl.BlockSpec((SWAP * SWAP, SWAP * SWAP), lambda i, gi: (0, 0)),
                  pl.BlockSpec((g, 2 * g), lambda i, gi: (0, 0))],
        out_specs=pl.BlockSpec((2, None, nb, SWAP, g),
                               lambda i, gi: (0, i // steps_per_batch, 0, i % steps_per_batch, gi)),
        out_shape=jax.ShapeDtypeStruct((2, batch, nb, na, chans), BF16),
        compiler_params=_params(("parallel", "parallel"), _vmem_limit(3 * rows * g * 2, 4 * rows * g * 4)),
        name="chan_dft",
    )(u, swap, w_chan.astype(BF16))

    jb = _blk(nb, TILES["dft_a_jb"], 1)
    t = pl.pallas_call(
        functools.partial(_dft_a_body, jb=jb, na=na),
        grid=(batch, nb // jb),
        in_specs=[pl.BlockSpec((2, None, jb, na, chans), lambda b, j: (0, b, j, 0, 0)),
                  pl.BlockSpec((2 * na, 2 * na), lambda b, j: (0, 0)),
                  pl.BlockSpec((jb, na, 1), lambda b, j: (j, 0, 0)),
                  pl.BlockSpec((jb, na, 1), lambda b, j: (j, 0, 0))],
        out_specs=pl.BlockSpec((None, 2, jb, na, chans), lambda b, j: (b, 0, j, 0, 0)),
        out_shape=jax.ShapeDtypeStruct((batch, 2, nb, na, chans), BF16),
        compiler_params=_params(("parallel", "parallel"),
                                _vmem_limit(4 * na * jb * chans * 2 + 2 * jb * na * LANES * 4,
                                            6 * na * chans * 4)),
        name="seq_dft_a",
    )(z, f_a.astype(BF16), tw_c, tw_s)

    cb = _blk(chans, TILES["dft_b_cols"], LANES)
    y = pl.pallas_call(
        functools.partial(_dft_b_body, nb=nb),
        grid=(batch, na // SWAP, chans // cb),
        in_specs=[pl.BlockSpec((None, 2, nb, SWAP, cb), lambda b, k, c: (b, 0, 0, k, c)),
                  pl.BlockSpec((SWAP * SWAP, SWAP * SWAP), lambda b, k, c: (0, 0)),
                  pl.BlockSpec((nb, 2 * nb), lambda b, k, c: (0, 0))],
        out_specs=pl.BlockSpec((None, nb, SWAP, cb), lambda b, k, c: (b, 0, k, c)),
        out_shape=jax.ShapeDtypeStruct((batch, nb, na, chans), BF16),
        compiler_params=_params(("parallel", "parallel", "parallel"),
                                _vmem_limit(3 * nb * SWAP * cb * 2, 10 * nb * SWAP * cb * 2)),
        name="seq_dft_b",
    )(t, swap, f_b.astype(BF16))
    return y.reshape(m, chans)


def _sigmoid(x):
    return 1.0 / (1.0 + jnp.exp(-x))


def _gate_mix_body(h_ref, a_ref, f_ref, wga_ref, wgf_ref, bga_ref, bgf_ref, wao_ref, wfo_ref, o_ref):
    h = h_ref[...]
    ga = _sigmoid(jnp.dot(h, wga_ref[...], preferred_element_type=F32) + bga_ref[...])
    gf = _sigmoid(jnp.dot(h, wgf_ref[...], preferred_element_type=F32) + bgf_ref[...])
    ya = jnp.dot(a_ref[...], wao_ref[...], preferred_element_type=F32)
    yf = jnp.dot(f_ref[...], wfo_ref[...], preferred_element_type=F32)
    o_ref[...] = (ga * ya + gf * yf).astype(o_ref.dtype)


def _gate_mix(h0b, attn, four, w_gate_b, b_gate, w_ao_b, w_fo_b):
    m, d = h0b.shape
    ka, kf = attn.shape[1], four.shape[1]
    bm = _blk(m, TILES["gate_bm"], BF16_SUBLANES)
    bn = _blk(d, TILES["gate_bn"], LANES)
    nb = d // bn
    bg = b_gate.reshape(1, 2 * d).astype(F32)
    blk_bytes = (bm * (d + ka + kf) + (2 * d + ka + kf) * bn + bm * bn) * 2
    return pl.pallas_call(
        _gate_mix_body,
        grid=(m // bm, nb),
        in_specs=[pl.BlockSpec((bm, d), lambda i, j: (i, 0)),
                  pl.BlockSpec((bm, ka), lambda i, j: (i, 0)),
                  pl.BlockSpec((bm, kf), lambda i, j: (i, 0)),
                  pl.BlockSpec((d, bn), lambda i, j: (0, j)),
                  pl.BlockSpec((d, bn), lambda i, j: (0, nb + j)),
                  pl.BlockSpec((1, bn), lambda i, j: (0, j)),
                  pl.BlockSpec((1, bn), lambda i, j: (0, nb + j)),
                  pl.BlockSpec((ka, bn), lambda i, j: (0, j)),
                  pl.BlockSpec((kf, bn), lambda i, j: (0, j))],
        out_specs=pl.BlockSpec((bm, bn), lambda i, j: (i, j)),
        out_shape=jax.ShapeDtypeStruct((m, d), BF16),
        compiler_params=_params(("parallel", "arbitrary"), _vmem_limit(blk_bytes, 6 * bm * bn * 4)),
        name="gate_mix",
    )(h0b, attn, four, w_gate_b, w_gate_b, bg, bg, w_ao_b, w_fo_b)


def _mm_res_body(x_ref, w_ref, r_ref, o_ref):
    o_ref[...] = ALPHA * r_ref[...] + jnp.dot(x_ref[...], w_ref[...], preferred_element_type=F32)


def _matmul_residual(x, w, res, bm_pref, bn_pref):
    m, k = x.shape
    n = w.shape[1]
    bm = _blk(m, bm_pref, BF16_SUBLANES)
    bn = _blk(n, bn_pref, LANES)
    blk_bytes = (bm * k + k * bn) * 2 + 2 * bm * bn * 4
    return pl.pallas_call(
        _mm_res_body,
        grid=(m // bm, n // bn),
        in_specs=[pl.BlockSpec((bm, k), lambda i, j: (i, 0)),
                  pl.BlockSpec((k, bn), lambda i, j: (0, j)),
                  pl.BlockSpec((bm, bn), lambda i, j: (i, j))],
        out_specs=pl.BlockSpec((bm, bn), lambda i, j: (i, j)),
        out_shape=jax.ShapeDtypeStruct((m, n), F32),
        compiler_params=_params(("parallel", "arbitrary"), _vmem_limit(blk_bytes, bm * bn * 4)),
        name="matmul_residual",
    )(x, w, res)


def _up_body(x_ref, xp_ref, xn_ref, wg_ref, wv_ref, cwg_ref, cwv_ref, cbg_ref, cbv_ref, o_ref, xs_ref,
             *, bm, blocks_per_seq, chunk):
    i = pl.program_id(0)
    j = pl.program_id(1)
    rows = bm + 2 * HALO

    @pl.when(j == 0)
    def _():
        pos = i % blocks_per_seq
        xs_ref[0:bm, :] = x_ref[...]
        xs_ref[bm:bm + HALO, :] = jnp.where(pos != blocks_per_seq - 1, xn_ref[...], jnp.zeros_like(xn_ref))
        xs_ref[bm + HALO:rows, :] = jnp.where(pos != 0, xp_ref[...], jnp.zeros_like(xp_ref))

    xs = xs_ref[...]

    def conv(w_ref, cw_ref, cb_ref, cols):
        a = jnp.dot(xs, w_ref[:, cols], preferred_element_type=F32)
        prev = pltpu.roll(a, 1, 0)[0:bm]
        nxt = pltpu.roll(a, rows - 1, 0)[0:bm]
        cw = cw_ref[:, cols]
        return prev * cw[0:1] + a[0:bm] * cw[1:2] + nxt * cw[2:3] + cb_ref[:, cols]

    for ch in range(o_ref.shape[1] // chunk):
        cols = slice(ch * chunk, (ch + 1) * chunk)
        gate = conv(wg_ref, cwg_ref, cbg_ref, cols)
        val = conv(wv_ref, cwv_ref, cbv_ref, cols)
        o_ref[:, cols] = (gate * _sigmoid(gate) * val).astype(o_ref.dtype)


def _conv_ffn_up(h1b, w_up_b, conv_w, conv_b, seq):
    m, d = h1b.shape
    dff = w_up_b.shape[1] // 2
    bm = _blk(seq, TILES["up_bm"], HALO)
    bn = _blk(dff, TILES["up_bn"], LANES)
    nb = dff // bn
    hb = bm // HALO
    last = m // HALO - 1
    cw = conv_w.astype(F32)
    cb = conv_b.reshape(1, 2 * dff).astype(F32)
    chunk = _blk(bn, TILES["up_chunk"], LANES)
    body = functools.partial(_up_body, bm=bm, blocks_per_seq=seq // bm, chunk=chunk)
    blk_bytes = (bm + 2 * HALO) * d * 2 + 2 * d * bn * 2 + bm * bn * 2 + 8 * bn * 4
    return pl.pallas_call(
        body,
        grid=(m // bm, nb),
        in_specs=[pl.BlockSpec((bm, d), lambda i, j: (i, 0)),
                  pl.BlockSpec((HALO, d), lambda i, j: (jnp.maximum(i * hb - 1, 0), 0)),
                  pl.BlockSpec((HALO, d), lambda i, j: (jnp.minimum((i + 1) * hb, last), 0)),
                  pl.BlockSpec((d, bn), lambda i, j: (0, j)),
                  pl.BlockSpec((d, bn), lambda i, j: (0, nb + j)),
                  pl.BlockSpec((CONV_WIDTH, bn), lambda i, j: (0, j)),
                  pl.BlockSpec((CONV_WIDTH, bn), lambda i, j: (0, nb + j)),
                  pl.BlockSpec((1, bn), lambda i, j: (0, j)),
                  pl.BlockSpec((1, bn), lambda i, j: (0, nb + j))],
        out_specs=pl.BlockSpec((bm, bn), lambda i, j: (i, j)),
        out_shape=jax.ShapeDtypeStruct((m, dff), BF16),
        scratch_shapes=[pltpu.VMEM((bm + 2 * HALO, d), BF16)],
        compiler_params=_params(("parallel", "arbitrary"),
                                _vmem_limit(blk_bytes, (bm + 2 * HALO) * (d * 2 + 16 * chunk * 4))),
        name="conv_ffn_up",
    )(h1b, h1b, h1b, w_up_b, w_up_b, cw, cw, cb, cb)


def kernel(x, ln_emb_g, ln_emb_b, w_in, lambda_q1, lambda_k1, lambda_q2, lambda_k2, subln_g, w_attn_o, w_fourier,
           w_gate, b_gate, w_mix_out, ln1_g, ln1_b, w_up, conv_w, conv_b, w_down, ln2_g, ln2_b):
    batch, seq, d = x.shape
    m = batch * seq
    depth = w_in.shape[0]
    assert depth == DEPTH
    attn_v = w_attn_o.shape[1]
    chans = w_fourier.shape[1]
    qk_cols = (w_in.shape[2] - attn_v - chans) // 2
    n_heads = attn_v // (2 * HEAD_DIM)
    assert qk_cols == attn_v and seq % SEQ_DFT_MAJOR == 0 and chans % FOURIER_GROUP == 0

    (h0, h0b) = _layer_norm(x.reshape(m, d), ln_emb_g, ln_emb_b, (F32, BF16))
    l = 0
    u = _in_proj(h0b, w_in[l].astype(BF16), seq, qk_cols)
    attn = _diff_attention(u, lambda_q1[l], lambda_k1[l], lambda_q2[l], lambda_k2[l], subln_g[l],
                           batch, seq, n_heads)
    four = _fourier_mix(u, batch, seq, 2 * qk_cols + attn_v, chans)
    mix = _gate_mix(h0b, attn, four, w_gate[l].astype(BF16), b_gate[l],
                    w_attn_o[l].astype(BF16), w_fourier[l].astype(BF16))
    s1 = _matmul_residual(mix, w_mix_out[l].astype(BF16), h0, TILES["mix_bm"], TILES["mix_bn"])
    (h1, h1b) = _layer_norm(s1, ln1_g[l], ln1_b[l], (F32, BF16))
    dff = w_down.shape[1]
    pad = -dff % FFN_PAD_MULTIPLE
    pad_cols = lambda a: jnp.pad(a, ((0, 0), (0, pad)))
    pad_halves = lambda a: jnp.concatenate([pad_cols(a[:, :dff]), pad_cols(a[:, dff:])], axis=1)
    w_up_b = pad_halves(w_up[l].astype(BF16))
    w_down_b = jnp.pad(w_down[l], ((0, pad), (0, 0))).astype(BF16)
    act = _conv_ffn_up(h1b, w_up_b, pad_halves(conv_w[l]), pad_halves(conv_b[l].reshape(1, -1)), seq)
    s2 = _matmul_residual(act, w_down_b, h1, TILES["down_bm"], TILES["down_bn"])
    (out,) = _layer_norm(s2, ln2_g[l], ln2_b[l], (F32,))
    return out.reshape(batch, seq, d)
```

```python
import functools
import math

import numpy as np
import jax
import jax.numpy as jnp
from jax import lax
from jax.experimental import pallas as pl
from jax.experimental.pallas import tpu as pltpu

F32 = jnp.float32
BF16 = jnp.bfloat16

HEAD_DIM = 128
ROT_DIM = HEAD_DIM // 4
ROPE_THETA = 500000.0
FOURIER_GROUP = 256
CONV_WIDTH = 3
DEPTH = 1
ALPHA = (2.0 * DEPTH) ** 0.25
LN_EPS = 1e-5
LAMBDA_INIT = 0.8 - 0.6 * math.exp(-0.3 * 0)

LANES = 128
BF16_SUBLANES = 16
F32_SUBLANES = 8
V7X_VMEM_BYTES = 64 * 1024 * 1024
VMEM_CAP_BYTES = 60000 * 1024
SEQ_DFT_MAJOR = 128
HALO = BF16_SUBLANES
SWAP = BF16_SUBLANES

TILES = dict(ln_rows=256, in_bm=1024, in_bn=1024, in_chunk=256, in_row_chunk=128,
             attn_tq=2048, attn_rows=256, attn_tk=1024, dft_a_jb=4, dft_b_cols=1024,
             gate_bm=512, gate_bn=512, mix_bm=1024, mix_bn=512, up_bm=1024, up_bn=512, up_chunk=256,
             down_bm=512, down_bn=256)
FFN_PAD_MULTIPLE = 1024


def _vmem_limit(pipelined_bytes, resident_bytes=0):
    need = 2 * pipelined_bytes + resident_bytes + (2 << 20)
    return int(min(max(need, 16 << 20), VMEM_CAP_BYTES))


def _blk(n, pref, align):
    if n <= pref:
        return n
    b = (pref // align) * align
    while b >= align:
        if n % b == 0:
            return b
        b -= align
    raise ValueError(f"no block of alignment {align} divides {n}")


def _params(sem, vmem):
    return pltpu.CompilerParams(dimension_semantics=sem, vmem_limit_bytes=vmem)


def _ln_stats(x):
    mu = jnp.mean(x, axis=-1, keepdims=True)
    xc = x - mu
    var = jnp.mean(xc * xc, axis=-1, keepdims=True)
    return mu, lax.rsqrt(var + LN_EPS)


def _ln_apply(x, mu, rstd, g, b):
    return (x - mu) * rstd * g + b


def _ln_body(x_ref, g_ref, b_ref, o_ref, *stat_refs):
    x = x_ref[...].astype(F32)
    mu, rstd = _ln_stats(x)
    o_ref[...] = _ln_apply(x, mu, rstd, g_ref[...], b_ref[...]).astype(o_ref.dtype)
    if stat_refs:
        stat_refs[0][...] = mu
        stat_refs[1][...] = rstd


def _layer_norm(x, g, b, out_dtype, with_stats):
    m, d = x.shape
    bm = _blk(m, TILES["ln_rows"], BF16_SUBLANES)
    row = pl.BlockSpec((bm, d), lambda i: (i, 0))
    vec = pl.BlockSpec((1, d), lambda i: (0, 0))
    stat = pl.BlockSpec((bm, 1), lambda i: (i, 0))
    n_stats = 2 if with_stats else 0
    blk_bytes = bm * d * (4 + jnp.dtype(out_dtype).itemsize) + n_stats * bm * LANES * 4
    return pl.pallas_call(
        _ln_body,
        grid=(m // bm,),
        in_specs=[row, vec, vec],
        out_specs=[row] + [stat] * n_stats,
        out_shape=[jax.ShapeDtypeStruct((m, d), out_dtype)] + [jax.ShapeDtypeStruct((m, 1), F32)] * n_stats,
        compiler_params=_params(("parallel",), _vmem_limit(blk_bytes, 4 * bm * d * 4)),
        name="layer_norm",
    )(x, g.reshape(1, d).astype(F32), b.reshape(1, d).astype(F32))


def _in_proj_body(x_ref, w_ref, cos_ref, sin_ref, o_ref, *, chunk, row_chunk):
    x = x_ref[...]
    bm, bn = o_ref.shape
    lane = lax.broadcasted_iota(jnp.int32, (row_chunk, LANES), 1)
    first_half = lane < (ROT_DIM // 2)
    for ch in range(bn // chunk):
        acc = jnp.dot(x, w_ref[:, ch * chunk:(ch + 1) * chunk], preferred_element_type=F32)
        for r in range(bm // row_chunk):
            rws = slice(r * row_chunk, (r + 1) * row_chunk)
            c, s = cos_ref[rws, :], sin_ref[rws, :]
            for cb in range(chunk // LANES):
                t = acc[rws, cb * LANES:(cb + 1) * LANES]
                partner = jnp.where(first_half,
                                    pltpu.roll(t, LANES - ROT_DIM // 2, 1),
                                    pltpu.roll(t, ROT_DIM // 2, 1))
                col0 = ch * chunk + cb * LANES
                o_ref[rws, col0:col0 + LANES] = (t * c + partner * s).astype(o_ref.dtype)


def _rotary_tables(seq, q_scale):
    inv_freq = ROPE_THETA ** (-jnp.arange(0, ROT_DIM, 2, dtype=F32) / ROT_DIM)
    ang = jnp.arange(seq, dtype=F32)[:, None] * inv_freq[None, :]
    cos, sin = jnp.cos(ang), jnp.sin(ang)
    pad = HEAD_DIM - ROT_DIM
    cos_k = jnp.concatenate([cos, cos, jnp.ones((seq, pad), F32)], axis=1)
    sin_k = jnp.concatenate([-sin, sin, jnp.zeros((seq, pad), F32)], axis=1)
    cos_t = jnp.stack([cos_k * q_scale, cos_k, jnp.ones_like(cos_k)])
    sin_t = jnp.stack([sin_k * q_scale, sin_k, jnp.zeros_like(sin_k)])
    return cos_t, sin_t


def _in_proj(h0b, w_in_b, seq, qk_cols):
    m, d = h0b.shape
    n = w_in_b.shape[1]
    bm = _blk(seq, TILES["in_bm"], BF16_SUBLANES)
    bn = _blk(qk_cols, TILES["in_bn"], LANES)
    chunk = _blk(bn, TILES["in_chunk"], LANES)
    row_chunk = _blk(bm, TILES["in_row_chunk"], BF16_SUBLANES)
    cos_t, sin_t = _rotary_tables(seq, HEAD_DIM ** -0.5 * math.log2(math.e))
    n_q_blocks = qk_cols // bn
    tab = pl.BlockSpec((None, bm, LANES), lambda i, j: (jnp.minimum(j // n_q_blocks, 2), i % (seq // bm), 0))
    body = functools.partial(_in_proj_body, chunk=chunk, row_chunk=row_chunk)
    blk_bytes = bm * d * 2 + d * bn * 2 + bm * bn * 2 + 2 * bm * LANES * 4
    return pl.pallas_call(
        body,
        grid=(m // bm, n // bn),
        in_specs=[pl.BlockSpec((bm, d), lambda i, j: (i, 0)),
                  pl.BlockSpec((d, bn), lambda i, j: (0, j)),
                  tab, tab],
        out_specs=pl.BlockSpec((bm, bn), lambda i, j: (i, j)),
        out_shape=jax.ShapeDtypeStruct((m, n), BF16),
        compiler_params=_params(("parallel", "arbitrary"), _vmem_limit(blk_bytes, 12 * bm * chunk * 4)),
        name="in_proj",
    )(h0b, w_in_b, cos_t, sin_t)


def _attn_body(q_ref, k_ref, v_ref, lq1_ref, lk1_ref, lq2_ref, lk2_ref, g_ref, o_ref,
               s0_ref, s1_ref, p0_ref, p1_ref, a0_ref, a1_ref, m_ref, l_ref, acc_ref, *, rows, tk, nr, nt):
    s_refs, p_refs, a_refs = (s0_ref, s1_ref), (p0_ref, p1_ref), (a0_ref, a1_ref)
    n_items = nr * nt
    m_ref[...] = jnp.full(m_ref.shape, -jnp.inf, F32)
    l_ref[...] = jnp.zeros(l_ref.shape, F32)
    acc_ref[...] = jnp.zeros(acc_ref.shape, F32)

    def offsets(i):
        if isinstance(i, int):
            return (i % nr) * rows, (i // nr) * tk
        return pl.multiple_of(lax.rem(i, nr) * rows, rows), pl.multiple_of(lax.div(i, nr) * tk, tk)

    def qk(i, slot):
        row0, key0 = offsets(i)
        for c in range(2):
            cols = slice(c * HEAD_DIM, (c + 1) * HEAD_DIM)
            s_refs[slot][c] = lax.dot_general(q_ref[pl.ds(row0, rows), cols], k_ref[pl.ds(key0, tk), cols],
                                              (((1,), (1,)), ((), ())), preferred_element_type=F32)

    def softmax(i, slot):
        row0, _ = offsets(i)
        for c in range(2):
            s = s_refs[slot][c]
            m_old = m_ref[c, pl.ds(row0, rows), :]
            m_new = jnp.maximum(m_old, jnp.max(s, axis=1, keepdims=True))
            a = jnp.exp2(m_old - m_new)
            p = jnp.exp2(s - m_new)
            l_ref[c, pl.ds(row0, rows), :] = a * l_ref[c, pl.ds(row0, rows), :] + jnp.sum(p, axis=1, keepdims=True)
            m_ref[c, pl.ds(row0, rows), :] = m_new
            p_refs[slot][c] = p.astype(p_refs[slot].dtype)
            a_refs[slot][c] = a

    def pv(i, slot):
        row0, key0 = offsets(i)
        v = v_ref[pl.ds(key0, tk), :]
        for c in range(2):
            acc_ref[c, pl.ds(row0, rows), :] = (a_refs[slot][c] * acc_ref[c, pl.ds(row0, rows), :]
                                                + jnp.dot(p_refs[slot][c], v, preferred_element_type=F32))

    qk(0, 0)
    qk(1, 1)
    softmax(0, 0)

    def pair(j, carry):
        i0 = 2 * j
        pv(i0 - 2, 0)
        qk(i0, 0)
        softmax(i0 - 1, 1)
        pv(i0 - 1, 1)
        qk(i0 + 1, 1)
        softmax(i0, 0)
        return carry

    lax.fori_loop(1, n_items // 2, pair, 0)
    pv(n_items - 2, 0)
    softmax(n_items - 1, 1)
    pv(n_items - 1, 1)

    lam = (jnp.exp(jnp.sum(lq1_ref[...] * lk1_ref[...], axis=1, keepdims=True))
           - jnp.exp(jnp.sum(lq2_ref[...] * lk2_ref[...], axis=1, keepdims=True))
           + LAMBDA_INIT)
    o = acc_ref[0] / l_ref[0] - lam * (acc_ref[1] / l_ref[1])
    ms = jnp.mean(o * o, axis=-1, keepdims=True)
    o = o * lax.rsqrt(ms + LN_EPS) * g_ref[...] * (1.0 - LAMBDA_INIT)
    o_ref[...] = o.astype(o_ref.dtype)


def _diff_attention(u, lq1, lk1, lq2, lk2, subln_g, batch, seq, n_heads):
    m = u.shape[0]
    hw = 2 * HEAD_DIM
    tq = _blk(seq, TILES["attn_tq"], BF16_SUBLANES)
    rows = _blk(tq, TILES["attn_rows"], BF16_SUBLANES)
    tk = _blk(seq, TILES["attn_tk"], LANES)
    nq, nr, nt = seq // tq, tq // rows, seq // tk
    assert (nr * nt) % 2 == 0 and nr * nt >= 4 and nr >= 2
    vec = lambda a: a.reshape(1, -1).astype(F32)
    small = pl.BlockSpec((1, HEAD_DIM), lambda b, h, qi: (0, 0))
    blk_bytes = 2 * tq * hw * 2 + 2 * seq * hw * 2
    scratch_bytes = 4 * rows * tk * (4 + 2) + 4 * rows * LANES * 4 + 2 * tq * (hw + 2 * LANES) * 4
    return pl.pallas_call(
        functools.partial(_attn_body, rows=rows, tk=tk, nr=nr, nt=nt),
        grid=(batch, n_heads, nq),
        in_specs=[pl.BlockSpec((tq, hw), lambda b, h, qi: (b * nq + qi, h)),
                  pl.BlockSpec((seq, hw), lambda b, h, qi: (b, n_heads + h)),
                  pl.BlockSpec((seq, hw), lambda b, h, qi: (b, 2 * n_heads + h)),
                  small, small, small, small,
                  pl.BlockSpec((1, hw), lambda b, h, qi: (0, 0))],
        out_specs=pl.BlockSpec((tq, hw), lambda b, h, qi: (b * nq + qi, h)),
        out_shape=jax.ShapeDtypeStruct((m, n_heads * hw), BF16),
        scratch_shapes=[pltpu.VMEM((2, rows, tk), F32), pltpu.VMEM((2, rows, tk), F32),
                        pltpu.VMEM((2, rows, tk), BF16), pltpu.VMEM((2, rows, tk), BF16),
                        pltpu.VMEM((2, rows, 1), F32), pltpu.VMEM((2, rows, 1), F32),
                        pltpu.VMEM((2, tq, 1), F32), pltpu.VMEM((2, tq, 1), F32),
                        pltpu.VMEM((2, tq, hw), F32)],
        compiler_params=_params(("parallel", "parallel", "arbitrary"),
                                _vmem_limit(blk_bytes, scratch_bytes + 2 * rows * tk * 4 + 6 * tq * hw * 4)),
        name="diff_attn",
    )(u, u, u, vec(lq1), vec(lk1), vec(lq2), vec(lk2), vec(subln_g))


def _dft_consts(seq):
    def cs(n, rows, cols):
        ph = 2.0 * np.pi * ((np.arange(rows)[:, None] * np.arange(cols)[None, :]) % n) / n
        return np.cos(ph), np.sin(ph)

    g = FOURIER_GROUP
    cc, sc = cs(g, g, g)
    w_chan = np.concatenate([cc, -sc], axis=1) / math.sqrt(g)
    na = SEQ_DFT_MAJOR
    nb = seq // na
    ca, sa = cs(na, na, na)
    f_a = np.block([[ca, sa], [-sa, ca]])
    tc, ts = cs(seq, nb, na)
    tw_c = (tc / math.sqrt(seq))[:, :, None]
    tw_s = (ts / math.sqrt(seq))[:, :, None]
    cb, sb = cs(nb, nb, nb)
    f_b = np.concatenate([cb, sb], axis=1)
    r = np.arange(SWAP * SWAP)
    swap = np.zeros((SWAP * SWAP, SWAP * SWAP))
    swap[r, (r % SWAP) * SWAP + r // SWAP] = 1.0
    as32 = lambda a: jnp.asarray(a.astype(np.float32))
    return as32(w_chan), as32(f_a), as32(tw_c), as32(tw_s), as32(f_b), as32(swap)


def _tiles(ref_or_val, starts):
    return jnp.concatenate([ref_or_val[s:s + SWAP] for s in starts], axis=0)


def _chan_dft_body(x_ref, swap_ref, w_ref, z_ref, *, nb):
    g = x_ref.shape[1]
    swap = swap_ref[...]
    w = w_ref[...]
    for n1 in range(nb // SWAP):
        xin = _tiles(x_ref, [a * nb + n1 * SWAP for a in range(SWAP)])
        xp = jnp.dot(swap, xin, preferred_element_type=F32).astype(x_ref.dtype)
        r = jnp.dot(xp, w, preferred_element_type=F32)
        for n2 in range(SWAP):
            rows = slice(n2 * SWAP, (n2 + 1) * SWAP)
            z_ref[0, n1 * SWAP + n2] = r[rows, :g].astype(z_ref.dtype)
            z_ref[1, n1 * SWAP + n2] = r[rows, g:].astype(z_ref.dtype)


def _dft_a_body(z_ref, fa_ref, twc_ref, tws_ref, t_ref, *, jb, na):
    fa = fa_ref[...]
    for jj in range(jb):
        zz = jnp.concatenate([z_ref[0, jj], z_ref[1, jj]], axis=0)
        r = jnp.dot(fa, zz, preferred_element_type=F32)
        tr, ti = r[:na], r[na:]
        c, s = twc_ref[jj], tws_ref[jj]
        t_ref[0, jj] = (tr * c + ti * s).astype(t_ref.dtype)
        t_ref[1, jj] = (ti * c - tr * s).astype(t_ref.dtype)


def _dft_b_body(t_ref, swap_ref, fb_ref, y_ref, *, nb):
    swap = swap_ref[...]
    fb = fb_ref[...]
    dt = t_ref.dtype
    q = []
    for p1 in range(2 * nb // SWAP):
        part, nb0 = divmod(p1 * SWAP, nb)
        xin = jnp.concatenate([t_ref[part, nb0 + p2] for p2 in range(SWAP)], axis=0)
        q.append(jnp.dot(swap, xin, preferred_element_type=F32).astype(dt))
    ys = []
    for ka in range(SWAP):
        m_ka = jnp.concatenate([qq[ka * SWAP:(ka + 1) * SWAP] for qq in q], axis=0)
        ys.append(jnp.dot(fb, m_ka, preferred_element_type=F32).astype(dt))
    for k1 in range(nb // SWAP):
        vin = jnp.concatenate([yy[k1 * SWAP:(k1 + 1) * SWAP] for yy in ys], axis=0)
        out = jnp.dot(swap, vin, preferred_element_type=F32).astype(y_ref.dtype)
        for k2 in range(SWAP):
            y_ref[k1 * SWAP + k2] = out[k2 * SWAP:(k2 + 1) * SWAP]


def _fourier_mix(u, batch, seq, f_col0, chans):
    m = u.shape[0]
    g = FOURIER_GROUP
    n_groups = chans // g
    w_chan, f_a, tw_c, tw_s, f_b, swap = _dft_consts(seq)
    na, nb = SEQ_DFT_MAJOR, seq // SEQ_DFT_MAJOR
    assert na % SWAP == 0 and nb % SWAP == 0
    swap = swap.astype(BF16)

    rows = SWAP * nb
    steps_per_batch = seq // rows
    z = pl.pallas_call(
        functools.partial(_chan_dft_body, nb=nb),
        grid=(m // rows, n_groups),
        in_specs=[pl.BlockSpec((rows, g), lambda i, gi: (i, f_col0 // g + gi)),
                  pl.BlockSpec((SWAP * SWAP, SWAP * SWAP), lambda i, gi: (0, 0)),
                  pl.BlockSpec((g, 2 * g), lambda i, gi: (0, 0))],
        out_specs=pl.BlockSpec((2, None, nb, SWAP, g),
                               lambda i, gi: (0, i // steps_per_batch, 0, i % steps_per_batch, gi)),
        out_shape=jax.ShapeDtypeStruct((2, batch, nb, na, chans), BF16),
        compiler_params=_params(("parallel", "parallel"), _vmem_limit(3 * rows * g * 2, 4 * rows * g * 4)),
        name="chan_dft",
    )(u, swap, w_chan.astype(BF16))

    jb = _blk(nb, TILES["dft_a_jb"], 1)
    t = pl.pallas_call(
        functools.partial(_dft_a_body, jb=jb, na=na),
        grid=(batch, nb // jb),
        in_specs=[pl.BlockSpec((2, None, jb, na, chans), lambda b, j: (0, b, j, 0, 0)),
                  pl.BlockSpec((2 * na, 2 * na), lambda b, j: (0, 0)),
                  pl.BlockSpec((jb, na, 1), lambda b, j: (j, 0, 0)),
                  pl.BlockSpec((jb, na, 1), lambda b, j: (j, 0, 0))],
        out_specs=pl.BlockSpec((None, 2, jb, na, chans), lambda b, j: (b, 0, j, 0, 0)),
        out_shape=jax.ShapeDtypeStruct((batch, 2, nb, na, chans), BF16),
        compiler_params=_params(("parallel", "parallel"),
                                _vmem_limit(4 * na * jb * chans * 2 + 2 * jb * na * LANES * 4,
                                            6 * na * chans * 4)),
        name="seq_dft_a",
    )(z, f_a.astype(BF16), tw_c, tw_s)

    cb = _blk(chans, TILES["dft_b_cols"], LANES)
    y = pl.pallas_call(
        functools.partial(_dft_b_body, nb=nb),
        grid=(batch, na // SWAP, chans // cb),
        in_specs=[pl.BlockSpec((None, 2, nb, SWAP, cb), lambda b, k, c: (b, 0, 0, k, c)),
                  pl.BlockSpec((SWAP * SWAP, SWAP * SWAP), lambda b, k, c: (0, 0)),
                  pl.BlockSpec((nb, 2 * nb), lambda b, k, c: (0, 0))],
        out_specs=pl.BlockSpec((None, nb, SWAP, cb), lambda b, k, c: (b, 0, k, c)),
        out_shape=jax.ShapeDtypeStruct((batch, nb, na, chans), BF16),
        compiler_params=_params(("parallel", "parallel", "parallel"),
                                _vmem_limit(3 * nb * SWAP * cb * 2, 10 * nb * SWAP * cb * 2)),
        name="seq_dft_b",
    )(t, swap, f_b.astype(BF16))
    return y.reshape(m, chans)


def _sigmoid(x):
    return 1.0 / (1.0 + jnp.exp(-x))


def _gate_mix_body(h_ref, a_ref, f_ref, wga_ref, wgf_ref, bga_ref, bgf_ref, wao_ref, wfo_ref, o_ref):
    h = h_ref[...]
    ga = _sigmoid(jnp.dot(h, wga_ref[...], preferred_element_type=F32) + bga_ref[...])
    gf = _sigmoid(jnp.dot(h, wgf_ref[...], preferred_element_type=F32) + bgf_ref[...])
    ya = jnp.dot(a_ref[...], wao_ref[...], preferred_element_type=F32)
    yf = jnp.dot(f_ref[...], wfo_ref[...], preferred_element_type=F32)
    o_ref[...] = (ga * ya + gf * yf).astype(o_ref.dtype)


def _gate_mix(h0b, attn, four, w_gate_b, b_gate, w_ao_b, w_fo_b):
    m, d = h0b.shape
    ka, kf = attn.shape[1], four.shape[1]
    bm = _blk(m, TILES["gate_bm"], BF16_SUBLANES)
    bn = _blk(d, TILES["gate_bn"], LANES)
    nb = d // bn
    bg = b_gate.reshape(1, 2 * d).astype(F32)
    blk_bytes = (bm * (d + ka + kf) + (2 * d + ka + kf) * bn + bm * bn) * 2
    return pl.pallas_call(
        _gate_mix_body,
        grid=(m // bm, nb),
        in_specs=[pl.BlockSpec((bm, d), lambda i, j: (i, 0)),
                  pl.BlockSpec((bm, ka), lambda i, j: (i, 0)),
                  pl.BlockSpec((bm, kf), lambda i, j: (i, 0)),
                  pl.BlockSpec((d, bn), lambda i, j: (0, j)),
                  pl.BlockSpec((d, bn), lambda i, j: (0, nb + j)),
                  pl.BlockSpec((1, bn), lambda i, j: (0, j)),
                  pl.BlockSpec((1, bn), lambda i, j: (0, nb + j)),
                  pl.BlockSpec((ka, bn), lambda i, j: (0, j)),
                  pl.BlockSpec((kf, bn), lambda i, j: (0, j))],
        out_specs=pl.BlockSpec((bm, bn), lambda i, j: (i, j)),
        out_shape=jax.ShapeDtypeStruct((m, d), BF16),
        compiler_params=_params(("parallel", "arbitrary"), _vmem_limit(blk_bytes, 6 * bm * bn * 4)),
        name="gate_mix",
    )(h0b, attn, four, w_gate_b, w_gate_b, bg, bg, w_ao_b, w_fo_b)


def _mm_res_body(x_ref, w_ref, r_ref, mu_ref, rstd_ref, g_ref, b_ref, o_ref):
    res = _ln_apply(r_ref[...], mu_ref[...], rstd_ref[...], g_ref[...], b_ref[...])
    o_ref[...] = ALPHA * res + jnp.dot(x_ref[...], w_ref[...], preferred_element_type=F32)


def _matmul_residual(x, w, pre, mu, rstd, g, b, bm_pref, bn_pref):
    m, k = x.shape
    n = w.shape[1]
    bm = _blk(m, bm_pref, BF16_SUBLANES)
    bn = _blk(n, bn_pref, LANES)
    blk_bytes = (bm * k + k * bn) * 2 + 2 * bm * bn * 4 + 2 * bm * LANES * 4
    tile = pl.BlockSpec((bm, bn), lambda i, j: (i, j))
    stat = pl.BlockSpec((bm, 1), lambda i, j: (i, 0))
    vec = pl.BlockSpec((1, bn), lambda i, j: (0, j))
    return pl.pallas_call(
        _mm_res_body,
        grid=(m // bm, n // bn),
        in_specs=[pl.BlockSpec((bm, k), lambda i, j: (i, 0)),
                  pl.BlockSpec((k, bn), lambda i, j: (0, j)),
                  tile, stat, stat, vec, vec],
        out_specs=tile,
        out_shape=jax.ShapeDtypeStruct((m, n), F32),
        compiler_params=_params(("parallel", "arbitrary"), _vmem_limit(blk_bytes, 2 * bm * bn * 4)),
        name="matmul_residual",
    )(x, w, pre, mu, rstd, g.reshape(1, n).astype(F32), b.reshape(1, n).astype(F32))


def _up_body(x_ref, xp_ref, xn_ref, wg_ref, wv_ref, cwg_ref, cwv_ref, cbg_ref, cbv_ref, o_ref, xs_ref,
             *, bm, blocks_per_seq, chunk):
    i = pl.program_id(0)
    j = pl.program_id(1)
    rows = bm + 2 * HALO

    @pl.when(j == 0)
    def _():
        pos = i % blocks_per_seq
        xs_ref[0:bm, :] = x_ref[...]
        xs_ref[bm:bm + HALO, :] = jnp.where(pos != blocks_per_seq - 1, xn_ref[...], jnp.zeros_like(xn_ref))
        xs_ref[bm + HALO:rows, :] = jnp.where(pos != 0, xp_ref[...], jnp.zeros_like(xp_ref))

    xs = xs_ref[...]

    def conv(w_ref, cw_ref, cb_ref, cols):
        a = jnp.dot(xs, w_ref[:, cols], preferred_element_type=F32)
        prev = pltpu.roll(a, 1, 0)[0:bm]
        nxt = pltpu.roll(a, rows - 1, 0)[0:bm]
        cw = cw_ref[:, cols]
        return prev * cw[0:1] + a[0:bm] * cw[1:2] + nxt * cw[2:3] + cb_ref[:, cols]

    for ch in range(o_ref.shape[1] // chunk):
        cols = slice(ch * chunk, (ch + 1) * chunk)
        gate = conv(wg_ref, cwg_ref, cbg_ref, cols)
        val = conv(wv_ref, cwv_ref, cbv_ref, cols)
        o_ref[:, cols] = (gate * _sigmoid(gate) * val).astype(o_ref.dtype)


def _conv_ffn_up(h1b, w_gate_b, w_val_b, cw_gate, cw_val, cb_gate, cb_val, seq):
    m, d = h1b.shape
    dff = w_gate_b.shape[1]
    bm = _blk(seq, TILES["up_bm"], HALO)
    bn = _blk(dff, TILES["up_bn"], LANES)
    nb = dff // bn
    hb = bm // HALO
    last = m // HALO - 1
    chunk = _blk(bn, TILES["up_chunk"], LANES)
    body = functools.partial(_up_body, bm=bm, blocks_per_seq=seq // bm, chunk=chunk)
    blk_bytes = (bm + 2 * HALO) * d * 2 + 2 * d * bn * 2 + bm * bn * 2 + 8 * bn * 4
    return pl.pallas_call(
        body,
        grid=(m // bm, nb),
        in_specs=[pl.BlockSpec((bm, d), lambda i, j: (i, 0)),
                  pl.BlockSpec((HALO, d), lambda i, j: (jnp.maximum(i * hb - 1, 0), 0)),
                  pl.BlockSpec((HALO, d), lambda i, j: (jnp.minimum((i + 1) * hb, last), 0)),
                  pl.BlockSpec((d, bn), lambda i, j: (0, j)),
                  pl.BlockSpec((d, bn), lambda i, j: (0, j)),
                  pl.BlockSpec((CONV_WIDTH, bn), lambda i, j: (0, j)),
                  pl.BlockSpec((CONV_WIDTH, bn), lambda i, j: (0, j)),
                  pl.BlockSpec((1, bn), lambda i, j: (0, j)),
                  pl.BlockSpec((1, bn), lambda i, j: (0, j))],
        out_specs=pl.BlockSpec((bm, bn), lambda i, j: (i, j)),
        out_shape=jax.ShapeDtypeStruct((m, dff), BF16),
        scratch_shapes=[pltpu.VMEM((bm + 2 * HALO, d), BF16)],
        compiler_params=_params(("parallel", "arbitrary"),
                                _vmem_limit(blk_bytes, (bm + 2 * HALO) * (d * 2 + 16 * chunk * 4))),
        name="conv_ffn_up",
    )(h1b, h1b, h1b, w_gate_b, w_val_b, cw_gate.astype(F32), cw_val.astype(F32),
      cb_gate.astype(F32), cb_val.astype(F32))


def kernel(x, ln_emb_g, ln_emb_b, w_in, lambda_q1, lambda_k1, lambda_q2, lambda_k2, subln_g, w_attn_o, w_fourier,
           w_gate, b_gate, w_mix_out, ln1_g, ln1_b, w_up, conv_w, conv_b, w_down, ln2_g, ln2_b):
    batch, seq, d = x.shape
    m = batch * seq
    depth = w_in.shape[0]
    assert depth == DEPTH
    attn_v = w_attn_o.shape[1]
    chans = w_fourier.shape[1]
    qk_cols = (w_in.shape[2] - attn_v - chans) // 2
    n_heads = attn_v // (2 * HEAD_DIM)
    assert qk_cols == attn_v and seq % SEQ_DFT_MAJOR == 0 and chans % FOURIER_GROUP == 0

    x2 = x.reshape(m, d)
    (h0b, mu0, rstd0) = _layer_norm(x2, ln_emb_g, ln_emb_b, BF16, True)
    l = 0
    u = _in_proj(h0b, w_in[l].astype(BF16), seq, qk_cols)
    attn = _diff_attention(u, lambda_q1[l], lambda_k1[l], lambda_q2[l], lambda_k2[l], subln_g[l],
                           batch, seq, n_heads)
    four = _fourier_mix(u, batch, seq, 2 * qk_cols + attn_v, chans)
    mix = _gate_mix(h0b, attn, four, w_gate[l].astype(BF16), b_gate[l],
                    w_attn_o[l].astype(BF16), w_fourier[l].astype(BF16))
    s1 = _matmul_residual(mix, w_mix_out[l].astype(BF16), x2, mu0, rstd0, ln_emb_g, ln_emb_b,
                          TILES["mix_bm"], TILES["mix_bn"])
    (h1b, mu1, rstd1) = _layer_norm(s1, ln1_g[l], ln1_b[l], BF16, True)
    dff = w_down.shape[1]
    pad = -dff % FFN_PAD_MULTIPLE
    gate_half = lambda a: jnp.pad(a[:, :dff], ((0, 0), (0, pad)))
    val_half = lambda a: jnp.pad(a[:, dff:], ((0, 0), (0, pad)))
    w_down_b = jnp.pad(w_down[l], ((0, pad), (0, 0))).astype(BF16)
    cb = conv_b[l].reshape(1, -1)
    act = _conv_ffn_up(h1b, gate_half(w_up[l]).astype(BF16), val_half(w_up[l]).astype(BF16),
                       gate_half(conv_w[l]), val_half(conv_w[l]), gate_half(cb), val_half(cb), seq)
    s2 = _matmul_residual(act, w_down_b, s1, mu1, rstd1, ln1_g[l], ln1_b[l], TILES["down_bm"], TILES["down_bn"])
    (out,) = _layer_norm(s2, ln2_g[l], ln2_b[l], F32, False)
    return out.reshape(batch, seq, d)
```

```python
import functools
import math

import numpy as np
import jax
import jax.numpy as jnp
from jax import lax
from jax.experimental import pallas as pl
from jax.experimental.pallas import tpu as pltpu

F32 = jnp.float32
BF16 = jnp.bfloat16

HEAD_DIM = 128
ROT_DIM = HEAD_DIM // 4
ROPE_THETA = 500000.0
FOURIER_GROUP = 256
CONV_WIDTH = 3
DEPTH = 1
ALPHA = (2.0 * DEPTH) ** 0.25
LN_EPS = 1e-5
LAMBDA_INIT = 0.8 - 0.6 * math.exp(-0.3 * 0)

LANES = 128
BF16_SUBLANES = 16
F32_SUBLANES = 8
V7X_VMEM_BYTES = 64 * 1024 * 1024
VMEM_CAP_BYTES = 60000 * 1024
SEQ_DFT_MAJOR = 128
HALO = BF16_SUBLANES
SWAP = BF16_SUBLANES

TILES = dict(ln_rows=256, in_bm=1024, in_bn=1024, in_chunk=256, in_row_chunk=128,
             attn_tq=2048, attn_rows=256, attn_tk=1024, dft_a_jb=4, dft_b_cols=1024,
             gate_bm=512, gate_bn=512, mix_bm=1024, mix_bn=512, up_bm=1024, up_bn=512, up_chunk=256,
             down_bm=256, down_bn=1024)
FFN_PAD_MULTIPLE = 1024


def _vmem_limit(pipelined_bytes, resident_bytes=0):
    need = 2 * pipelined_bytes + resident_bytes + (2 << 20)
    return int(min(max(need, 16 << 20), VMEM_CAP_BYTES))


def _blk(n, pref, align):
    if n <= pref:
        return n
    b = (pref // align) * align
    while b >= align:
        if n % b == 0:
            return b
        b -= align
    raise ValueError(f"no block of alignment {align} divides {n}")


def _params(sem, vmem):
    return pltpu.CompilerParams(dimension_semantics=sem, vmem_limit_bytes=vmem)


def _ln_stats(x):
    mu = jnp.mean(x, axis=-1, keepdims=True)
    xc = x - mu
    var = jnp.mean(xc * xc, axis=-1, keepdims=True)
    return mu, lax.rsqrt(var + LN_EPS)


def _ln_apply(x, mu, rstd, g, b):
    return (x - mu) * rstd * g + b


def _ln_body(x_ref, g_ref, b_ref, o_ref, *stat_refs):
    x = x_ref[...].astype(F32)
    mu, rstd = _ln_stats(x)
    o_ref[...] = _ln_apply(x, mu, rstd, g_ref[...], b_ref[...]).astype(o_ref.dtype)
    if stat_refs:
        stat_refs[0][...] = mu
        stat_refs[1][...] = rstd


def _layer_norm(x, g, b, out_dtype, with_stats):
    m, d = x.shape
    bm = _blk(m, TILES["ln_rows"], BF16_SUBLANES)
    row = pl.BlockSpec((bm, d), lambda i: (i, 0))
    vec = pl.BlockSpec((1, d), lambda i: (0, 0))
    stat = pl.BlockSpec((bm, 1), lambda i: (i, 0))
    n_stats = 2 if with_stats else 0
    blk_bytes = bm * d * (4 + jnp.dtype(out_dtype).itemsize) + n_stats * bm * LANES * 4
    return pl.pallas_call(
        _ln_body,
        grid=(m // bm,),
        in_specs=[row, vec, vec],
        out_specs=[row] + [stat] * n_stats,
        out_shape=[jax.ShapeDtypeStruct((m, d), out_dtype)] + [jax.ShapeDtypeStruct((m, 1), F32)] * n_stats,
        compiler_params=_params(("parallel",), _vmem_limit(blk_bytes, 4 * bm * d * 4)),
        name="layer_norm",
    )(x, g.reshape(1, d).astype(F32), b.reshape(1, d).astype(F32))


def _in_proj_body(x_ref, w_ref, cos_ref, sin_ref, o_ref, *, chunk, row_chunk):
    x = x_ref[...]
    bm, bn = o_ref.shape
    lane = lax.broadcasted_iota(jnp.int32, (row_chunk, LANES), 1)
    first_half = lane < (ROT_DIM // 2)
    for ch in range(bn // chunk):
        acc = jnp.dot(x, w_ref[:, ch * chunk:(ch + 1) * chunk], preferred_element_type=F32)
        for r in range(bm // row_chunk):
            rws = slice(r * row_chunk, (r + 1) * row_chunk)
            c, s = cos_ref[rws, :], sin_ref[rws, :]
            for cb in range(chunk // LANES):
                t = acc[rws, cb * LANES:(cb + 1) * LANES]
                partner = jnp.where(first_half,
                                    pltpu.roll(t, LANES - ROT_DIM // 2, 1),
                                    pltpu.roll(t, ROT_DIM // 2, 1))
                col0 = ch * chunk + cb * LANES
                o_ref[rws, col0:col0 + LANES] = (t * c + partner * s).astype(o_ref.dtype)


def _rotary_tables(seq, q_scale):
    inv_freq = ROPE_THETA ** (-jnp.arange(0, ROT_DIM, 2, dtype=F32) / ROT_DIM)
    ang = jnp.arange(seq, dtype=F32)[:, None] * inv_freq[None, :]
    cos, sin = jnp.cos(ang), jnp.sin(ang)
    pad = HEAD_DIM - ROT_DIM
    cos_k = jnp.concatenate([cos, cos, jnp.ones((seq, pad), F32)], axis=1)
    sin_k = jnp.concatenate([-sin, sin, jnp.zeros((seq, pad), F32)], axis=1)
    cos_t = jnp.stack([cos_k * q_scale, cos_k, jnp.ones_like(cos_k)])
    sin_t = jnp.stack([sin_k * q_scale, sin_k, jnp.zeros_like(sin_k)])
    return cos_t, sin_t


def _in_proj(h0b, w_in_b, seq, qk_cols):
    m, d = h0b.shape
    n = w_in_b.shape[1]
    bm = _blk(seq, TILES["in_bm"], BF16_SUBLANES)
    bn = _blk(qk_cols, TILES["in_bn"], LANES)
    chunk = _blk(bn, TILES["in_chunk"], LANES)
    row_chunk = _blk(bm, TILES["in_row_chunk"], BF16_SUBLANES)
    cos_t, sin_t = _rotary_tables(seq, HEAD_DIM ** -0.5 * math.log2(math.e))
    n_q_blocks = qk_cols // bn
    tab = pl.BlockSpec((None, bm, LANES), lambda i, j: (jnp.minimum(j // n_q_blocks, 2), i % (seq // bm), 0))
    body = functools.partial(_in_proj_body, chunk=chunk, row_chunk=row_chunk)
    blk_bytes = bm * d * 2 + d * bn * 2 + bm * bn * 2 + 2 * bm * LANES * 4
    return pl.pallas_call(
        body,
        grid=(m // bm, n // bn),
        in_specs=[pl.BlockSpec((bm, d), lambda i, j: (i, 0)),
                  pl.BlockSpec((d, bn), lambda i, j: (0, j)),
                  tab, tab],
        out_specs=pl.BlockSpec((bm, bn), lambda i, j: (i, j)),
        out_shape=jax.ShapeDtypeStruct((m, n), BF16),
        compiler_params=_params(("parallel", "arbitrary"), _vmem_limit(blk_bytes, 12 * bm * chunk * 4)),
        name="in_proj",
    )(h0b, w_in_b, cos_t, sin_t)


def _attn_body(q_ref, k_ref, v_ref, lq1_ref, lk1_ref, lq2_ref, lk2_ref, g_ref, o_ref,
               s0_ref, s1_ref, p0_ref, p1_ref, a0_ref, a1_ref, m_ref, l_ref, acc_ref, *, rows, tk, nr, nt):
    s_refs, p_refs, a_refs = (s0_ref, s1_ref), (p0_ref, p1_ref), (a0_ref, a1_ref)
    n_items = nr * nt
    m_ref[...] = jnp.full(m_ref.shape, -jnp.inf, F32)
    l_ref[...] = jnp.zeros(l_ref.shape, F32)
    acc_ref[...] = jnp.zeros(acc_ref.shape, F32)

    def offsets(i):
        if isinstance(i, int):
            return (i % nr) * rows, (i // nr) * tk
        return pl.multiple_of(lax.rem(i, nr) * rows, rows), pl.multiple_of(lax.div(i, nr) * tk, tk)

    def qk(i, slot):
        row0, key0 = offsets(i)
        for c in range(2):
            cols = slice(c * HEAD_DIM, (c + 1) * HEAD_DIM)
            s_refs[slot][c] = lax.dot_general(q_ref[pl.ds(row0, rows), cols], k_ref[pl.ds(key0, tk), cols],
                                              (((1,), (1,)), ((), ())), preferred_element_type=F32)

    def softmax(i, slot):
        row0, _ = offsets(i)
        for c in range(2):
            s = s_refs[slot][c]
            m_old = m_ref[c, pl.ds(row0, rows), :]
            m_new = jnp.maximum(m_old, jnp.max(s, axis=1, keepdims=True))
            a = jnp.exp2(m_old - m_new)
            p = jnp.exp2(s - m_new)
            l_ref[c, pl.ds(row0, rows), :] = a * l_ref[c, pl.ds(row0, rows), :] + jnp.sum(p, axis=1, keepdims=True)
            m_ref[c, pl.ds(row0, rows), :] = m_new
            p_refs[slot][c] = p.astype(p_refs[slot].dtype)
            a_refs[slot][c] = a

    def pv(i, slot):
        row0, key0 = offsets(i)
        v = v_ref[pl.ds(key0, tk), :]
        for c in range(2):
            acc_ref[c, pl.ds(row0, rows), :] = (a_refs[slot][c] * acc_ref[c, pl.ds(row0, rows), :]
                                                + jnp.dot(p_refs[slot][c], v, preferred_element_type=F32))

    qk(0, 0)
    qk(1, 1)
    softmax(0, 0)

    def pair(j, carry):
        i0 = 2 * j
        pv(i0 - 2, 0)
        qk(i0, 0)
        softmax(i0 - 1, 1)
        pv(i0 - 1, 1)
        qk(i0 + 1, 1)
        softmax(i0, 0)
        return carry

    lax.fori_loop(1, n_items // 2, pair, 0)
    pv(n_items - 2, 0)
    softmax(n_items - 1, 1)
    pv(n_items - 1, 1)

    lam = (jnp.exp(jnp.sum(lq1_ref[...] * lk1_ref[...], axis=1, keepdims=True))
           - jnp.exp(jnp.sum(lq2_ref[...] * lk2_ref[...], axis=1, keepdims=True))
           + LAMBDA_INIT)
    o = acc_ref[0] / l_ref[0] - lam * (acc_ref[1] / l_ref[1])
    ms = jnp.mean(o * o, axis=-1, keepdims=True)
    o = o * lax.rsqrt(ms + LN_EPS) * g_ref[...] * (1.0 - LAMBDA_INIT)
    o_ref[...] = o.astype(o_ref.dtype)


def _diff_attention(u, lq1, lk1, lq2, lk2, subln_g, batch, seq, n_heads):
    m = u.shape[0]
    hw = 2 * HEAD_DIM
    tq = _blk(seq, TILES["attn_tq"], BF16_SUBLANES)
    rows = _blk(tq, TILES["attn_rows"], BF16_SUBLANES)
    tk = _blk(seq, TILES["attn_tk"], LANES)
    nq, nr, nt = seq // tq, tq // rows, seq // tk
    assert (nr * nt) % 2 == 0 and nr * nt >= 4 and nr >= 2
    vec = lambda a: a.reshape(1, -1).astype(F32)
    small = pl.BlockSpec((1, HEAD_DIM), lambda b, h, qi: (0, 0))
    blk_bytes = 2 * tq * hw * 2 + 2 * seq * hw * 2
    scratch_bytes = 4 * rows * tk * (4 + 2) + 4 * rows * LANES * 4 + 2 * tq * (hw + 2 * LANES) * 4
    return pl.pallas_call(
        functools.partial(_attn_body, rows=rows, tk=tk, nr=nr, nt=nt),
        grid=(batch, n_heads, nq),
        in_specs=[pl.BlockSpec((tq, hw), lambda b, h, qi: (b * nq + qi, h)),
                  pl.BlockSpec((seq, hw), lambda b, h, qi: (b, n_heads + h)),
                  pl.BlockSpec((seq, hw), lambda b, h, qi: (b, 2 * n_heads + h)),
                  small, small, small, small,
                  pl.BlockSpec((1, hw), lambda b, h, qi: (0, 0))],
        out_specs=pl.BlockSpec((tq, hw), lambda b, h, qi: (b * nq + qi, h)),
        out_shape=jax.ShapeDtypeStruct((m, n_heads * hw), BF16),
        scratch_shapes=[pltpu.VMEM((2, rows, tk), F32), pltpu.VMEM((2, rows, tk), F32),
                        pltpu.VMEM((2, rows, tk), BF16), pltpu.VMEM((2, rows, tk), BF16),
                        pltpu.VMEM((2, rows, 1), F32), pltpu.VMEM((2, rows, 1), F32),
                        pltpu.VMEM((2, tq, 1), F32), pltpu.VMEM((2, tq, 1), F32),
                        pltpu.VMEM((2, tq, hw), F32)],
        compiler_params=_params(("parallel", "parallel", "arbitrary"),
                                _vmem_limit(blk_bytes, scratch_bytes + 2 * rows * tk * 4 + 6 * tq * hw * 4)),
        name="diff_attn",
    )(u, u, u, vec(lq1), vec(lk1), vec(lq2), vec(lk2), vec(subln_g))


def _dft_consts(seq):
    def cs(n, rows, cols):
        ph = 2.0 * np.pi * ((np.arange(rows)[:, None] * np.arange(cols)[None, :]) % n) / n
        return np.cos(ph), np.sin(ph)

    g = FOURIER_GROUP
    cc, sc = cs(g, g, g)
    w_chan = np.concatenate([cc, -sc], axis=1) / math.sqrt(g)
    na = SEQ_DFT_MAJOR
    nb = seq // na
    ca, sa = cs(na, na, na)
    f_a = np.block([[ca, sa], [-sa, ca]])
    tc, ts = cs(seq, nb, na)
    tw_c = (tc / math.sqrt(seq))[:, :, None]
    tw_s = (ts / math.sqrt(seq))[:, :, None]
    cb, sb = cs(nb, nb, nb)
    f_b = np.concatenate([cb, sb], axis=1)
    r = np.arange(SWAP * SWAP)
    swap = np.zeros((SWAP * SWAP, SWAP * SWAP))
    swap[r, (r % SWAP) * SWAP + r // SWAP] = 1.0
    as32 = lambda a: jnp.asarray(a.astype(np.float32))
    return as32(w_chan), as32(f_a), as32(tw_c), as32(tw_s), as32(f_b), as32(swap)


def _tiles(ref_or_val, starts):
    return jnp.concatenate([ref_or_val[s:s + SWAP] for s in starts], axis=0)


def _chan_dft_body(x_ref, swap_ref, w_ref, z_ref, *, nb):
    g = x_ref.shape[1]
    swap = swap_ref[...]
    w = w_ref[...]
    for n1 in range(nb // SWAP):
        xin = _tiles(x_ref, [a * nb + n1 * SWAP for a in range(SWAP)])
        xp = jnp.dot(swap, xin, preferred_element_type=F32).astype(x_ref.dtype)
        r = jnp.dot(xp, w, preferred_element_type=F32)
        for n2 in range(SWAP):
            rows = slice(n2 * SWAP, (n2 + 1) * SWAP)
            z_ref[0, n1 * SWAP + n2] = r[rows, :g].astype(z_ref.dtype)
            z_ref[1, n1 * SWAP + n2] = r[rows, g:].astype(z_ref.dtype)


def _dft_a_body(z_ref, fa_ref, twc_ref, tws_ref, t_ref, *, jb, na):
    fa = fa_ref[...]
    for jj in range(jb):
        zz = jnp.concatenate([z_ref[0, jj], z_ref[1, jj]], axis=0)
        r = jnp.dot(fa, zz, preferred_element_type=F32)
        tr, ti = r[:na], r[na:]
        c, s = twc_ref[jj], tws_ref[jj]
        t_ref[0, jj] = (tr * c + ti * s).astype(t_ref.dtype)
        t_ref[1, jj] = (ti * c - tr * s).astype(t_ref.dtype)


def _dft_b_body(t_ref, swap_ref, fb_ref, y_ref, *, nb):
    swap = swap_ref[...]
    fb = fb_ref[...]
    dt = t_ref.dtype
    q = []
    for p1 in range(2 * nb // SWAP):
        part, nb0 = divmod(p1 * SWAP, nb)
        xin = jnp.concatenate([t_ref[part, nb0 + p2] for p2 in range(SWAP)], axis=0)
        q.append(jnp.dot(swap, xin, preferred_element_type=F32).astype(dt))
    ys = []
    for ka in range(SWAP):
        m_ka = jnp.concatenate([qq[ka * SWAP:(ka + 1) * SWAP] for qq in q], axis=0)
        ys.append(jnp.dot(fb, m_ka, preferred_element_type=F32).astype(dt))
    for k1 in range(nb // SWAP):
        vin = jnp.concatenate([yy[k1 * SWAP:(k1 + 1) * SWAP] for yy in ys], axis=0)
        out = jnp.dot(swap, vin, preferred_element_type=F32).astype(y_ref.dtype)
        for k2 in range(SWAP):
            y_ref[k1 * SWAP + k2] = out[k2 * SWAP:(k2 + 1) * SWAP]


def _fourier_mix(u, batch, seq, f_col0, chans):
    m = u.shape[0]
    g = FOURIER_GROUP
    n_groups = chans // g
    w_chan, f_a, tw_c, tw_s, f_b, swap = _dft_consts(seq)
    na, nb = SEQ_DFT_MAJOR, seq // SEQ_DFT_MAJOR
    assert na % SWAP == 0 and nb % SWAP == 0
    swap = swap.astype(BF16)

    rows = SWAP * nb
    steps_per_batch = seq // rows
    z = pl.pallas_call(
        functools.partial(_chan_dft_body, nb=nb),
        grid=(m // rows, n_groups),
        in_specs=[pl.BlockSpec((rows, g), lambda i, gi: (i, f_col0 // g + gi)),
                  pl.BlockSpec((SWAP * SWAP, SWAP * SWAP), lambda i, gi: (0, 0)),
                  pl.BlockSpec((g, 2 * g), lambda i, gi: (0, 0))],
        out_specs=pl.BlockSpec((2, None, nb, SWAP, g),
                               lambda i, gi: (0, i // steps_per_batch, 0, i % steps_per_batch, gi)),
        out_shape=jax.ShapeDtypeStruct((2, batch, nb, na, chans), BF16),
        compiler_params=_params(("parallel", "parallel"), _vmem_limit(3 * rows * g * 2, 4 * rows * g * 4)),
        name="chan_dft",
    )(u, swap, w_chan.astype(BF16))

    jb = _blk(nb, TILES["dft_a_jb"], 1)
    t = pl.pallas_call(
        functools.partial(_dft_a_body, jb=jb, na=na),
        grid=(batch, nb // jb),
        in_specs=[pl.BlockSpec((2, None, jb, na, chans), lambda b, j: (0, b, j, 0, 0)),
                  pl.BlockSpec((2 * na, 2 * na), lambda b, j: (0, 0)),
                  pl.BlockSpec((jb, na, 1), lambda b, j: (j, 0, 0)),
                  pl.BlockSpec((jb, na, 1), lambda b, j: (j, 0, 0))],
        out_specs=pl.BlockSpec((None, 2, jb, na, chans), lambda b, j: (b, 0, j, 0, 0)),
        out_shape=jax.ShapeDtypeStruct((batch, 2, nb, na, chans), BF16),
        compiler_params=_params(("parallel", "parallel"),
                                _vmem_limit(4 * na * jb * chans * 2 + 2 * jb * na * LANES * 4,
                                            6 * na * chans * 4)),
        name="seq_dft_a",
    )(z, f_a.astype(BF16), tw_c, tw_s)

    cb = _blk(chans, TILES["dft_b_cols"], LANES)
    y = pl.pallas_call(
        functools.partial(_dft_b_body, nb=nb),
        grid=(batch, na // SWAP, chans // cb),
        in_specs=[pl.BlockSpec((None, 2, nb, SWAP, cb), lambda b, k, c: (b, 0, 0, k, c)),
                  pl.BlockSpec((SWAP * SWAP, SWAP * SWAP), lambda b, k, c: (0, 0)),
                  pl.BlockSpec((nb, 2 * nb), lambda b, k, c: (0, 0))],
        out_specs=pl.BlockSpec((None, nb, SWAP, cb), lambda b, k, c: (b, 0, k, c)),
        out_shape=jax.ShapeDtypeStruct((batch, nb, na, chans), BF16),
        compiler_params=_params(("parallel", "parallel", "parallel"),
                                _vmem_limit(3 * nb * SWAP * cb * 2, 10 * nb * SWAP * cb * 2)),
        name="seq_dft_b",
    )(t, swap, f_b.astype(BF16))
    return y.reshape(m, chans)


def _sigmoid(x):
    return 1.0 / (1.0 + jnp.exp(-x))


def _gate_mix_body(h_ref, a_ref, f_ref, wga_ref, wgf_ref, bga_ref, bgf_ref, wao_ref, wfo_ref, o_ref):
    h = h_ref[...]
    ga = _sigmoid(jnp.dot(h, wga_ref[...], preferred_element_type=F32) + bga_ref[...])
    gf = _sigmoid(jnp.dot(h, wgf_ref[...], preferred_element_type=F32) + bgf_ref[...])
    ya = jnp.dot(a_ref[...], wao_ref[...], preferred_element_type=F32)
    yf = jnp.dot(f_ref[...], wfo_ref[...], preferred_element_type=F32)
    o_ref[...] = (ga * ya + gf * yf).astype(o_ref.dtype)


def _gate_mix(h0b, attn, four, w_gate_b, b_gate, w_ao_b, w_fo_b):
    m, d = h0b.shape
    ka, kf = attn.shape[1], four.shape[1]
    bm = _blk(m, TILES["gate_bm"], BF16_SUBLANES)
    bn = _blk(d, TILES["gate_bn"], LANES)
    nb = d // bn
    bg = b_gate.reshape(1, 2 * d).astype(F32)
    blk_bytes = (bm * (d + ka + kf) + (2 * d + ka + kf) * bn + bm * bn) * 2
    return pl.pallas_call(
        _gate_mix_body,
        grid=(m // bm, nb),
        in_specs=[pl.BlockSpec((bm, d), lambda i, j: (i, 0)),
                  pl.BlockSpec((bm, ka), lambda i, j: (i, 0)),
                  pl.BlockSpec((bm, kf), lambda i, j: (i, 0)),
                  pl.BlockSpec((d, bn), lambda i, j: (0, j)),
                  pl.BlockSpec((d, bn), lambda i, j: (0, nb + j)),
                  pl.BlockSpec((1, bn), lambda i, j: (0, j)),
                  pl.BlockSpec((1, bn), lambda i, j: (0, nb + j)),
                  pl.BlockSpec((ka, bn), lambda i, j: (0, j)),
                  pl.BlockSpec((kf, bn), lambda i, j: (0, j))],
        out_specs=pl.BlockSpec((bm, bn), lambda i, j: (i, j)),
        out_shape=jax.ShapeDtypeStruct((m, d), BF16),
        compiler_params=_params(("parallel", "arbitrary"), _vmem_limit(blk_bytes, 6 * bm * bn * 4)),
        name="gate_mix",
    )(h0b, attn, four, w_gate_b, w_gate_b, bg, bg, w_ao_b, w_fo_b)


def _mm_res_body(x_ref, w_ref, r_ref, mu_ref, rstd_ref, g_ref, b_ref, o_ref):
    res = _ln_apply(r_ref[...], mu_ref[...], rstd_ref[...], g_ref[...], b_ref[...])
    o_ref[...] = ALPHA * res + jnp.dot(x_ref[...], w_ref[...], preferred_element_type=F32)


def _matmul_residual(x, w, pre, mu, rstd, g, b, bm_pref, bn_pref, weights_resident):
    m, k = x.shape
    n = w.shape[1]
    bm = _blk(m, bm_pref, BF16_SUBLANES)
    bn = _blk(n, bn_pref, LANES)
    if weights_resident:
        grid = (n // bn, m // bm)
        row, col = (lambda j, i: i), (lambda j, i: j)
        w_mode = {"pipeline_mode": pl.Buffered(1)}
        blk_bytes = bm * k * 2 + 2 * bm * bn * 4 + 2 * bm * LANES * 4
        resident_bytes = k * bn * 2 + 6 * bm * bn * 4
    else:
        grid = (m // bm, n // bn)
        row, col = (lambda i, j: i), (lambda i, j: j)
        w_mode = {}
        blk_bytes = (bm * k + k * bn) * 2 + 2 * bm * bn * 4 + 2 * bm * LANES * 4
        resident_bytes = 2 * bm * bn * 4
    tile = pl.BlockSpec((bm, bn), lambda a, c: (row(a, c), col(a, c)))
    stat = pl.BlockSpec((bm, 1), lambda a, c: (row(a, c), 0))
    vec = pl.BlockSpec((1, bn), lambda a, c: (0, col(a, c)))
    return pl.pallas_call(
        _mm_res_body,
        grid=grid,
        in_specs=[pl.BlockSpec((bm, k), lambda a, c: (row(a, c), 0)),
                  pl.BlockSpec((k, bn), lambda a, c: (0, col(a, c)), **w_mode),
                  tile, stat, stat, vec, vec],
        out_specs=tile,
        out_shape=jax.ShapeDtypeStruct((m, n), F32),
        compiler_params=_params(("parallel", "arbitrary"), _vmem_limit(blk_bytes, resident_bytes)),
        name="matmul_residual",
    )(x, w, pre, mu, rstd, g.reshape(1, n).astype(F32), b.reshape(1, n).astype(F32))


def _up_body(x_ref, xp_ref, xn_ref, wg_ref, wv_ref, cwg_ref, cwv_ref, cbg_ref, cbv_ref, o_ref, xs_ref,
             *, bm, blocks_per_seq, chunk):
    i = pl.program_id(0)
    j = pl.program_id(1)
    rows = bm + 2 * HALO

    @pl.when(j == 0)
    def _():
        pos = i % blocks_per_seq
        xs_ref[0:bm, :] = x_ref[...]
        xs_ref[bm:bm + HALO, :] = jnp.where(pos != blocks_per_seq - 1, xn_ref[...], jnp.zeros_like(xn_ref))
        xs_ref[bm + HALO:rows, :] = jnp.where(pos != 0, xp_ref[...], jnp.zeros_like(xp_ref))

    xs = xs_ref[...]

    def conv(w_ref, cw_ref, cb_ref, cols):
        a = jnp.dot(xs, w_ref[:, cols], preferred_element_type=F32)
        prev = pltpu.roll(a, 1, 0)[0:bm]
        nxt = pltpu.roll(a, rows - 1, 0)[0:bm]
        cw = cw_ref[:, cols]
        return prev * cw[0:1] + a[0:bm] * cw[1:2] + nxt * cw[2:3] + cb_ref[:, cols]

    for ch in range(o_ref.shape[1] // chunk):
        cols = slice(ch * chunk, (ch + 1) * chunk)
        gate = conv(wg_ref, cwg_ref, cbg_ref, cols)
        val = conv(wv_ref, cwv_ref, cbv_ref, cols)
        o_ref[:, cols] = (gate * _sigmoid(gate) * val).astype(o_ref.dtype)


def _conv_ffn_up(h1b, w_gate_b, w_val_b, cw_gate, cw_val, cb_gate, cb_val, seq):
    m, d = h1b.shape
    dff = w_gate_b.shape[1]
    bm = _blk(seq, TILES["up_bm"], HALO)
    bn = _blk(dff, TILES["up_bn"], LANES)
    nb = dff // bn
    hb = bm // HALO
    last = m // HALO - 1
    chunk = _blk(bn, TILES["up_chunk"], LANES)
    body = functools.partial(_up_body, bm=bm, blocks_per_seq=seq // bm, chunk=chunk)
    blk_bytes = (bm + 2 * HALO) * d * 2 + 2 * d * bn * 2 + bm * bn * 2 + 8 * bn * 4
    return pl.pallas_call(
        body,
        grid=(m // bm, nb),
        in_specs=[pl.BlockSpec((bm, d), lambda i, j: (i, 0)),
                  pl.BlockSpec((HALO, d), lambda i, j: (jnp.maximum(i * hb - 1, 0), 0)),
                  pl.BlockSpec((HALO, d), lambda i, j: (jnp.minimum((i + 1) * hb, last), 0)),
                  pl.BlockSpec((d, bn), lambda i, j: (0, j)),
                  pl.BlockSpec((d, bn), lambda i, j: (0, j)),
                  pl.BlockSpec((CONV_WIDTH, bn), lambda i, j: (0, j)),
                  pl.BlockSpec((CONV_WIDTH, bn), lambda i, j: (0, j)),
                  pl.BlockSpec((1, bn), lambda i, j: (0, j)),
                  pl.BlockSpec((1, bn), lambda i, j: (0, j))],
        out_specs=pl.BlockSpec((bm, bn), lambda i, j: (i, j)),
        out_shape=jax.ShapeDtypeStruct((m, dff), BF16),
        scratch_shapes=[pltpu.VMEM((bm + 2 * HALO, d), BF16)],
        compiler_params=_params(("parallel", "arbitrary"),
                                _vmem_limit(blk_bytes, (bm + 2 * HALO) * (d * 2 + 16 * chunk * 4))),
        name="conv_ffn_up",
    )(h1b, h1b, h1b, w_gate_b, w_val_b, cw_gate.astype(F32), cw_val.astype(F32),
      cb_gate.astype(F32), cb_val.astype(F32))


def kernel(x, ln_emb_g, ln_emb_b, w_in, lambda_q1, lambda_k1, lambda_q2, lambda_k2, subln_g, w_attn_o, w_fourier,
           w_gate, b_gate, w_mix_out, ln1_g, ln1_b, w_up, conv_w, conv_b, w_down, ln2_g, ln2_b):
    batch, seq, d = x.shape
    m = batch * seq
    depth = w_in.shape[0]
    assert depth == DEPTH
    attn_v = w_attn_o.shape[1]
    chans = w_fourier.shape[1]
    qk_cols = (w_in.shape[2] - attn_v - chans) // 2
    n_heads = attn_v // (2 * HEAD_DIM)
    assert qk_cols == attn_v and seq % SEQ_DFT_MAJOR == 0 and chans % FOURIER_GROUP == 0

    x2 = x.reshape(m, d)
    (h0b, mu0, rstd0) = _layer_norm(x2, ln_emb_g, ln_emb_b, BF16, True)
    l = 0
    u = _in_proj(h0b, w_in[l].astype(BF16), seq, qk_cols)
    attn = _diff_attention(u, lambda_q1[l], lambda_k1[l], lambda_q2[l], lambda_k2[l], subln_g[l],
                           batch, seq, n_heads)
    four = _fourier_mix(u, batch, seq, 2 * qk_cols + attn_v, chans)
    mix = _gate_mix(h0b, attn, four, w_gate[l].astype(BF16), b_gate[l],
                    w_attn_o[l].astype(BF16), w_fourier[l].astype(BF16))
    s1 = _matmul_residual(mix, w_mix_out[l].astype(BF16), x2, mu0, rstd0, ln_emb_g, ln_emb_b,
                          TILES["mix_bm"], TILES["mix_bn"], False)
    (h1b, mu1, rstd1) = _layer_norm(s1, ln1_g[l], ln1_b[l], BF16, True)
    dff = w_down.shape[1]
    pad = -dff % FFN_PAD_MULTIPLE
    gate_half = lambda a: jnp.pad(a[:, :dff], ((0, 0), (0, pad)))
    val_half = lambda a: jnp.pad(a[:, dff:], ((0, 0), (0, pad)))
    w_down_b = jnp.pad(w_down[l], ((0, pad), (0, 0))).astype(BF16)
    cb = conv_b[l].reshape(1, -1)
    act = _conv_ffn_up(h1b, gate_half(w_up[l]).astype(BF16), val_half(w_up[l]).astype(BF16),
                       gate_half(conv_w[l]), val_half(conv_w[l]), gate_half(cb), val_half(cb), seq)
    s2 = _matmul_residual(act, w_down_b, s1, mu1, rstd1, ln1_g[l], ln1_b[l],
                          TILES["down_bm"], TILES["down_bn"], True)
    (out,) = _layer_norm(s2, ln2_g[l], ln2_b[l], F32, False)
    return out.reshape(batch, seq, d)
```

```python
import functools
import math

import numpy as np
import jax
import jax.numpy as jnp
from jax import lax
from jax.experimental import pallas as pl
from jax.experimental.pallas import tpu as pltpu

F32 = jnp.float32
BF16 = jnp.bfloat16

HEAD_DIM = 128
ROT_DIM = HEAD_DIM // 4
ROPE_THETA = 500000.0
FOURIER_GROUP = 256
CONV_WIDTH = 3
DEPTH = 1
ALPHA = (2.0 * DEPTH) ** 0.25
LN_EPS = 1e-5
LAMBDA_INIT = 0.8 - 0.6 * math.exp(-0.3 * 0)

LANES = 128
BF16_SUBLANES = 16
VMEM_CAP_BYTES = 60000 * 1024
SEQ_DFT_MAJOR = 128
HALO = BF16_SUBLANES
SWAP = BF16_SUBLANES

TILES = dict(ln_rows=256, in_bm=1024, in_bn=1024, in_chunk=256, in_row_chunk=128,
             attn_tq=2048, attn_rows=256, attn_tk=1024, dft_a_jb=4, dft_b_cols=1024,
             gate_bm=512, gate_bn=512, mix_bm=1024, mix_bn=512, up_bm=1024, up_bn=512, up_chunk=256,
             down_bm=256, down_bn=1024)
FFN_PAD_MULTIPLE = 1024


def _vmem_limit(pipelined_bytes, resident_bytes=0):
    need = 2 * pipelined_bytes + resident_bytes + (2 << 20)
    return int(min(max(need, 16 << 20), VMEM_CAP_BYTES))


def _blk(n, pref, align):
    if n <= pref:
        return n
    b = (pref // align) * align
    while b >= align:
        if n % b == 0:
            return b
        b -= align
    raise ValueError(f"no block of alignment {align} divides {n}")


def _params(sem, vmem):
    return pltpu.CompilerParams(dimension_semantics=sem, vmem_limit_bytes=vmem)


def _ln_stats(x):
    mu = jnp.mean(x, axis=-1, keepdims=True)
    xc = x - mu
    var = jnp.mean(xc * xc, axis=-1, keepdims=True)
    return mu, lax.rsqrt(var + LN_EPS)


def _ln_apply(x, mu, rstd, g, b):
    return (x - mu) * rstd * g + b


def _ln_body(x_ref, g_ref, b_ref, o_ref, *stat_refs):
    x = x_ref[...].astype(F32)
    mu, rstd = _ln_stats(x)
    o_ref[...] = _ln_apply(x, mu, rstd, g_ref[...], b_ref[...]).astype(o_ref.dtype)
    if stat_refs:
        stat_refs[0][...] = mu
        stat_refs[1][...] = rstd


def _layer_norm(x, g, b, out_dtype, with_stats):
    m, d = x.shape
    bm = _blk(m, TILES["ln_rows"], BF16_SUBLANES)
    row = pl.BlockSpec((bm, d), lambda i: (i, 0))
    vec = pl.BlockSpec((1, d), lambda i: (0, 0))
    stat = pl.BlockSpec((bm, 1), lambda i: (i, 0))
    n_stats = 2 if with_stats else 0
    blk_bytes = bm * d * (4 + jnp.dtype(out_dtype).itemsize) + n_stats * bm * LANES * 4
    return pl.pallas_call(
        _ln_body,
        grid=(m // bm,),
        in_specs=[row, vec, vec],
        out_specs=[row] + [stat] * n_stats,
        out_shape=[jax.ShapeDtypeStruct((m, d), out_dtype)] + [jax.ShapeDtypeStruct((m, 1), F32)] * n_stats,
        compiler_params=_params(("parallel",), _vmem_limit(blk_bytes, 4 * bm * d * 4)),
        name="layer_norm",
    )(x, g.reshape(1, d).astype(F32), b.reshape(1, d).astype(F32))


def _in_proj_body(x_ref, w_ref, cos_ref, sin_ref, o_ref, *, chunk, row_chunk):
    x = x_ref[...]
    bm, bn = o_ref.shape
    lane = lax.broadcasted_iota(jnp.int32, (row_chunk, LANES), 1)
    first_half = lane < (ROT_DIM // 2)
    for ch in range(bn // chunk):
        acc = jnp.dot(x, w_ref[:, ch * chunk:(ch + 1) * chunk], preferred_element_type=F32)
        for r in range(bm // row_chunk):
            rws = slice(r * row_chunk, (r + 1) * row_chunk)
            c, s = cos_ref[rws, :], sin_ref[rws, :]
            for cb in range(chunk // LANES):
                t = acc[rws, cb * LANES:(cb + 1) * LANES]
                partner = jnp.where(first_half,
                                    pltpu.roll(t, LANES - ROT_DIM // 2, 1),
                                    pltpu.roll(t, ROT_DIM // 2, 1))
                col0 = ch * chunk + cb * LANES
                o_ref[rws, col0:col0 + LANES] = (t * c + partner * s).astype(o_ref.dtype)


def _rotary_tables(seq, q_scale):
    inv_freq = ROPE_THETA ** (-jnp.arange(0, ROT_DIM, 2, dtype=F32) / ROT_DIM)
    ang = jnp.arange(seq, dtype=F32)[:, None] * inv_freq[None, :]
    cos, sin = jnp.cos(ang), jnp.sin(ang)
    pad = HEAD_DIM - ROT_DIM
    cos_k = jnp.concatenate([cos, cos, jnp.ones((seq, pad), F32)], axis=1)
    sin_k = jnp.concatenate([-sin, sin, jnp.zeros((seq, pad), F32)], axis=1)
    cos_t = jnp.stack([cos_k * q_scale, cos_k, jnp.ones_like(cos_k)])
    sin_t = jnp.stack([sin_k * q_scale, sin_k, jnp.zeros_like(sin_k)])
    return cos_t, sin_t


def _in_proj(h0b, w_in_b, seq, qk_cols):
    m, d = h0b.shape
    n = w_in_b.shape[1]
    bm = _blk(seq, TILES["in_bm"], BF16_SUBLANES)
    bn = _blk(qk_cols, TILES["in_bn"], LANES)
    chunk = _blk(bn, TILES["in_chunk"], LANES)
    row_chunk = _blk(bm, TILES["in_row_chunk"], BF16_SUBLANES)
    cos_t, sin_t = _rotary_tables(seq, HEAD_DIM ** -0.5 * math.log2(math.e))
    n_q_blocks = qk_cols // bn
    tab = pl.BlockSpec((None, bm, LANES), lambda i, j: (jnp.minimum(j // n_q_blocks, 2), i % (seq // bm), 0))
    body = functools.partial(_in_proj_body, chunk=chunk, row_chunk=row_chunk)
    blk_bytes = bm * d * 2 + d * bn * 2 + bm * bn * 2 + 2 * bm * LANES * 4
    return pl.pallas_call(
        body,
        grid=(m // bm, n // bn),
        in_specs=[pl.BlockSpec((bm, d), lambda i, j: (i, 0)),
                  pl.BlockSpec((d, bn), lambda i, j: (0, j)),
                  tab, tab],
        out_specs=pl.BlockSpec((bm, bn), lambda i, j: (i, j)),
        out_shape=jax.ShapeDtypeStruct((m, n), BF16),
        compiler_params=_params(("parallel", "arbitrary"), _vmem_limit(blk_bytes, 12 * bm * chunk * 4)),
        name="in_proj",
    )(h0b, w_in_b, cos_t, sin_t)


def _attn_body(q_ref, k_ref, v_ref, lq1_ref, lk1_ref, lq2_ref, lk2_ref, g_ref, o_ref,
               s0_ref, s1_ref, p0_ref, p1_ref, a0_ref, a1_ref, m_ref, l_ref, acc_ref, *, rows, tk, nr, nt):
    s_refs, p_refs, a_refs = (s0_ref, s1_ref), (p0_ref, p1_ref), (a0_ref, a1_ref)
    n_items = nr * nt
    m_ref[...] = jnp.full(m_ref.shape, -jnp.inf, F32)
    l_ref[...] = jnp.zeros(l_ref.shape, F32)
    acc_ref[...] = jnp.zeros(acc_ref.shape, F32)

    def offsets(i):
        if isinstance(i, int):
            return (i % nr) * rows, (i // nr) * tk
        return pl.multiple_of(lax.rem(i, nr) * rows, rows), pl.multiple_of(lax.div(i, nr) * tk, tk)

    def qk(i, slot):
        row0, key0 = offsets(i)
        for c in range(2):
            cols = slice(c * HEAD_DIM, (c + 1) * HEAD_DIM)
            s_refs[slot][c] = lax.dot_general(q_ref[pl.ds(row0, rows), cols], k_ref[pl.ds(key0, tk), cols],
                                              (((1,), (1,)), ((), ())), preferred_element_type=F32)

    def softmax(i, slot):
        row0, _ = offsets(i)
        for c in range(2):
            s = s_refs[slot][c]
            m_old = m_ref[c, pl.ds(row0, rows), :]
            m_new = jnp.maximum(m_old, jnp.max(s, axis=1, keepdims=True))
            a = jnp.exp2(m_old - m_new)
            p = jnp.exp2(s - m_new)
            l_ref[c, pl.ds(row0, rows), :] = a * l_ref[c, pl.ds(row0, rows), :] + jnp.sum(p, axis=1, keepdims=True)
            m_ref[c, pl.ds(row0, rows), :] = m_new
            p_refs[slot][c] = p.astype(p_refs[slot].dtype)
            a_refs[slot][c] = a

    def pv(i, slot):
        row0, key0 = offsets(i)
        v = v_ref[pl.ds(key0, tk), :]
        for c in range(2):
            acc_ref[c, pl.ds(row0, rows), :] = (a_refs[slot][c] * acc_ref[c, pl.ds(row0, rows), :]
                                                + jnp.dot(p_refs[slot][c], v, preferred_element_type=F32))

    qk(0, 0)
    qk(1, 1)
    softmax(0, 0)

    def pair(j, carry):
        i0 = 2 * j
        pv(i0 - 2, 0)
        qk(i0, 0)
        softmax(i0 - 1, 1)
        pv(i0 - 1, 1)
        qk(i0 + 1, 1)
        softmax(i0, 0)
        return carry

    lax.fori_loop(1, n_items // 2, pair, 0)
    pv(n_items - 2, 0)
    softmax(n_items - 1, 1)
    pv(n_items - 1, 1)

    lam = (jnp.exp(jnp.sum(lq1_ref[...] * lk1_ref[...], axis=1, keepdims=True))
           - jnp.exp(jnp.sum(lq2_ref[...] * lk2_ref[...], axis=1, keepdims=True))
           + LAMBDA_INIT)
    o = acc_ref[0] / l_ref[0] - lam * (acc_ref[1] / l_ref[1])
    ms = jnp.mean(o * o, axis=-1, keepdims=True)
    o = o * lax.rsqrt(ms + LN_EPS) * g_ref[...] * (1.0 - LAMBDA_INIT)
    o_ref[...] = o.astype(o_ref.dtype)


def _diff_attention(u, lq1, lk1, lq2, lk2, subln_g, batch, seq, n_heads):
    m = u.shape[0]
    hw = 2 * HEAD_DIM
    tq = _blk(seq, TILES["attn_tq"], BF16_SUBLANES)
    rows = _blk(tq, TILES["attn_rows"], BF16_SUBLANES)
    tk = _blk(seq, TILES["attn_tk"], LANES)
    nq, nr, nt = seq // tq, tq // rows, seq // tk
    assert (nr * nt) % 2 == 0 and nr * nt >= 4 and nr >= 2
    vec = lambda a: a.reshape(1, -1).astype(F32)
    small = pl.BlockSpec((1, HEAD_DIM), lambda b, h, qi: (0, 0))
    blk_bytes = 2 * tq * hw * 2 + 2 * seq * hw * 2
    scratch_bytes = 4 * rows * tk * (4 + 2) + 4 * rows * LANES * 4 + 2 * tq * (hw + 2 * LANES) * 4
    return pl.pallas_call(
        functools.partial(_attn_body, rows=rows, tk=tk, nr=nr, nt=nt),
        grid=(batch, n_heads, nq),
        in_specs=[pl.BlockSpec((tq, hw), lambda b, h, qi: (b * nq + qi, h)),
                  pl.BlockSpec((seq, hw), lambda b, h, qi: (b, n_heads + h)),
                  pl.BlockSpec((seq, hw), lambda b, h, qi: (b, 2 * n_heads + h)),
                  small, small, small, small,
                  pl.BlockSpec((1, hw), lambda b, h, qi: (0, 0))],
        out_specs=pl.BlockSpec((tq, hw), lambda b, h, qi: (b * nq + qi, h)),
        out_shape=jax.ShapeDtypeStruct((m, n_heads * hw), BF16),
        scratch_shapes=[pltpu.VMEM((2, rows, tk), F32), pltpu.VMEM((2, rows, tk), F32),
                        pltpu.VMEM((2, rows, tk), BF16), pltpu.VMEM((2, rows, tk), BF16),
                        pltpu.VMEM((2, rows, 1), F32), pltpu.VMEM((2, rows, 1), F32),
                        pltpu.VMEM((2, tq, 1), F32), pltpu.VMEM((2, tq, 1), F32),
                        pltpu.VMEM((2, tq, hw), F32)],
        compiler_params=_params(("parallel", "parallel", "arbitrary"),
                                _vmem_limit(blk_bytes, scratch_bytes + 2 * rows * tk * 4 + 6 * tq * hw * 4)),
        name="diff_attn",
    )(u, u, u, vec(lq1), vec(lk1), vec(lq2), vec(lk2), vec(subln_g))


def _dft_consts(seq):
    def cs(n, rows, cols):
        ph = 2.0 * np.pi * ((np.arange(rows)[:, None] * np.arange(cols)[None, :]) % n) / n
        return np.cos(ph), np.sin(ph)

    g = FOURIER_GROUP
    cc, sc = cs(g, g, g)
    w_chan = np.concatenate([cc, -sc], axis=1) / math.sqrt(g)
    na = SEQ_DFT_MAJOR
    nb = seq // na
    ca, sa = cs(na, na, na)
    f_a = np.block([[ca, sa], [-sa, ca]])
    tc, ts = cs(seq, nb, na)
    tw_c = (tc / math.sqrt(seq))[:, :, None]
    tw_s = (ts / math.sqrt(seq))[:, :, None]
    cb, sb = cs(nb, nb, nb)
    f_b = np.concatenate([cb, sb], axis=1)
    r = np.arange(SWAP * SWAP)
    swap = np.zeros((SWAP * SWAP, SWAP * SWAP))
    swap[r, (r % SWAP) * SWAP + r // SWAP] = 1.0
    as32 = lambda a: jnp.asarray(a.astype(np.float32))
    return as32(w_chan), as32(f_a), as32(tw_c), as32(tw_s), as32(f_b), as32(swap)


def _tiles(ref_or_val, starts):
    return jnp.concatenate([ref_or_val[s:s + SWAP] for s in starts], axis=0)


def _chan_dft_body(x_ref, swap_ref, w_ref, z_ref, *, nb):
    g = x_ref.shape[1]
    swap = swap_ref[...]
    w = w_ref[...]
    for n1 in range(nb // SWAP):
        xin = _tiles(x_ref, [a * nb + n1 * SWAP for a in range(SWAP)])
        xp = jnp.dot(swap, xin, preferred_element_type=F32).astype(x_ref.dtype)
        r = jnp.dot(xp, w, preferred_element_type=F32)
        for n2 in range(SWAP):
            rows = slice(n2 * SWAP, (n2 + 1) * SWAP)
            z_ref[0, n1 * SWAP + n2] = r[rows, :g].astype(z_ref.dtype)
            z_ref[1, n1 * SWAP + n2] = r[rows, g:].astype(z_ref.dtype)


def _dft_a_body(z_ref, fa_ref, twc_ref, tws_ref, t_ref, *, jb, na):
    fa = fa_ref[...]
    for jj in range(jb):
        zz = jnp.concatenate([z_ref[0, jj], z_ref[1, jj]], axis=0)
        r = jnp.dot(fa, zz, preferred_element_type=F32)
        tr, ti = r[:na], r[na:]
        c, s = twc_ref[jj], tws_ref[jj]
        t_ref[0, jj] = (tr * c + ti * s).astype(t_ref.dtype)
        t_ref[1, jj] = (ti * c - tr * s).astype(t_ref.dtype)


def _dft_b_body(t_ref, swap_ref, fb_ref, y_ref, *, nb):
    swap = swap_ref[...]
    fb = fb_ref[...]
    dt = t_ref.dtype
    q = []
    for p1 in range(2 * nb // SWAP):
        part, nb0 = divmod(p1 * SWAP, nb)
        xin = jnp.concatenate([t_ref[part, nb0 + p2] for p2 in range(SWAP)], axis=0)
        q.append(jnp.dot(swap, xin, preferred_element_type=F32).astype(dt))
    ys = []
    for ka in range(SWAP):
        m_ka = jnp.concatenate([qq[ka * SWAP:(ka + 1) * SWAP] for qq in q], axis=0)
        ys.append(jnp.dot(fb, m_ka, preferred_element_type=F32).astype(dt))
    for k1 in range(nb // SWAP):
        vin = jnp.concatenate([yy[k1 * SWAP:(k1 + 1) * SWAP] for yy in ys], axis=0)
        out = jnp.dot(swap, vin, preferred_element_type=F32).astype(y_ref.dtype)
        for k2 in range(SWAP):
            y_ref[k1 * SWAP + k2] = out[k2 * SWAP:(k2 + 1) * SWAP]


def _fourier_mix(u, batch, seq, f_col0, chans):
    m = u.shape[0]
    g = FOURIER_GROUP
    n_groups = chans // g
    w_chan, f_a, tw_c, tw_s, f_b, swap = _dft_consts(seq)
    na, nb = SEQ_DFT_MAJOR, seq // SEQ_DFT_MAJOR
    assert na % SWAP == 0 and nb % SWAP == 0
    swap = swap.astype(BF16)

    rows = SWAP * nb
    steps_per_batch = seq // rows
    z = pl.pallas_call(
        functools.partial(_chan_dft_body, nb=nb),
        grid=(m // rows, n_groups),
        in_specs=[pl.BlockSpec((rows, g), lambda i, gi: (i, f_col0 // g + gi)),
                  pl.BlockSpec((SWAP * SWAP, SWAP * SWAP), lambda i, gi: (0, 0)),
                  pl.BlockSpec((g, 2 * g), lambda i, gi: (0, 0))],
        out_specs=pl.BlockSpec((2, None, nb, SWAP, g),
                               lambda i, gi: (0, i // steps_per_batch, 0, i % steps_per_batch, gi)),
        out_shape=jax.ShapeDtypeStruct((2, batch, nb, na, chans), BF16),
        compiler_params=_params(("parallel", "parallel"), _vmem_limit(3 * rows * g * 2, 4 * rows * g * 4)),
        name="chan_dft",
    )(u, swap, w_chan.astype(BF16))

    jb = _blk(nb, TILES["dft_a_jb"], 1)
    t = pl.pallas_call(
        functools.partial(_dft_a_body, jb=jb, na=na),
        grid=(batch, nb // jb),
        in_specs=[pl.BlockSpec((2, None, jb, na, chans), lambda b, j: (0, b, j, 0, 0)),
                  pl.BlockSpec((2 * na, 2 * na), lambda b, j: (0, 0)),
                  pl.BlockSpec((jb, na, 1), lambda b, j: (j, 0, 0)),
                  pl.BlockSpec((jb, na, 1), lambda b, j: (j, 0, 0))],
        out_specs=pl.BlockSpec((None, 2, jb, na, chans), lambda b, j: (b, 0, j, 0, 0)),
        out_shape=jax.ShapeDtypeStruct((batch, 2, nb, na, chans), BF16),
        compiler_params=_params(("parallel", "parallel"),
                                _vmem_limit(4 * na * jb * chans * 2 + 2 * jb * na * LANES * 4,
                                            6 * na * chans * 4)),
        name="seq_dft_a",
    )(z, f_a.astype(BF16), tw_c, tw_s)

    cb = _blk(chans, TILES["dft_b_cols"], LANES)
    y = pl.pallas_call(
        functools.partial(_dft_b_body, nb=nb),
        grid=(batch, na // SWAP, chans // cb),
        in_specs=[pl.BlockSpec((None, 2, nb, SWAP, cb), lambda b, k, c: (b, 0, 0, k, c)),
                  pl.BlockSpec((SWAP * SWAP, SWAP * SWAP), lambda b, k, c: (0, 0)),
                  pl.BlockSpec((nb, 2 * nb), lambda b, k, c: (0, 0))],
        out_specs=pl.BlockSpec((None, nb, SWAP, cb), lambda b, k, c: (b, 0, k, c)),
        out_shape=jax.ShapeDtypeStruct((batch, nb, na, chans), BF16),
        compiler_params=_params(("parallel", "parallel", "parallel"),
                                _vmem_limit(3 * nb * SWAP * cb * 2, 10 * nb * SWAP * cb * 2)),
        name="seq_dft_b",
    )(t, swap, f_b.astype(BF16))
    return y.reshape(m, chans)


def _sigmoid(x):
    return 1.0 / (1.0 + jnp.exp(-x))


def _gate_mix_body(h_ref, a_ref, f_ref, wga_ref, wgf_ref, bga_ref, bgf_ref, wao_ref, wfo_ref, o_ref):
    h = h_ref[...]
    ga = _sigmoid(jnp.dot(h, wga_ref[...], preferred_element_type=F32) + bga_ref[...])
    gf = _sigmoid(jnp.dot(h, wgf_ref[...], preferred_element_type=F32) + bgf_ref[...])
    ya = jnp.dot(a_ref[...], wao_ref[...], preferred_element_type=F32)
    yf = jnp.dot(f_ref[...], wfo_ref[...], preferred_element_type=F32)
    o_ref[...] = (ga * ya + gf * yf).astype(o_ref.dtype)


def _gate_mix(h0b, attn, four, w_gate_b, b_gate, w_ao_b, w_fo_b):
    m, d = h0b.shape
    ka, kf = attn.shape[1], four.shape[1]
    bm = _blk(m, TILES["gate_bm"], BF16_SUBLANES)
    bn = _blk(d, TILES["gate_bn"], LANES)
    nb = d // bn
    bg = b_gate.reshape(1, 2 * d).astype(F32)
    blk_bytes = (bm * (d + ka + kf) + (2 * d + ka + kf) * bn + bm * bn) * 2
    return pl.pallas_call(
        _gate_mix_body,
        grid=(m // bm, nb),
        in_specs=[pl.BlockSpec((bm, d), lambda i, j: (i, 0)),
                  pl.BlockSpec((bm, ka), lambda i, j: (i, 0)),
                  pl.BlockSpec((bm, kf), lambda i, j: (i, 0)),
                  pl.BlockSpec((d, bn), lambda i, j: (0, j)),
                  pl.BlockSpec((d, bn), lambda i, j: (0, nb + j)),
                  pl.BlockSpec((1, bn), lambda i, j: (0, j)),
                  pl.BlockSpec((1, bn), lambda i, j: (0, nb + j)),
                  pl.BlockSpec((ka, bn), lambda i, j: (0, j)),
                  pl.BlockSpec((kf, bn), lambda i, j: (0, j))],
        out_specs=pl.BlockSpec((bm, bn), lambda i, j: (i, j)),
        out_shape=jax.ShapeDtypeStruct((m, d), BF16),
        compiler_params=_params(("parallel", "arbitrary"), _vmem_limit(blk_bytes, 6 * bm * bn * 4)),
        name="gate_mix",
    )(h0b, attn, four, w_gate_b, w_gate_b, bg, bg, w_ao_b, w_fo_b)


def _mm_res_body(x_ref, w_ref, r_ref, mu_ref, rstd_ref, g_ref, b_ref, o_ref):
    res = _ln_apply(r_ref[...], mu_ref[...], rstd_ref[...], g_ref[...], b_ref[...])
    o_ref[...] = ALPHA * res + jnp.dot(x_ref[...], w_ref[...], preferred_element_type=F32)


def _matmul_residual(x, w, pre, mu, rstd, g, b, bm_pref, bn_pref, weights_resident):
    m, k = x.shape
    n = w.shape[1]
    bm = _blk(m, bm_pref, BF16_SUBLANES)
    bn = _blk(n, bn_pref, LANES)
    if weights_resident:
        grid = (n // bn, m // bm)
        row, col = (lambda j, i: i), (lambda j, i: j)
        w_mode = {"pipeline_mode": pl.Buffered(1)}
        blk_bytes = bm * k * 2 + 2 * bm * bn * 4 + 2 * bm * LANES * 4
        resident_bytes = k * bn * 2 + 6 * bm * bn * 4
    else:
        grid = (m // bm, n // bn)
        row, col = (lambda i, j: i), (lambda i, j: j)
        w_mode = {}
        blk_bytes = (bm * k + k * bn) * 2 + 2 * bm * bn * 4 + 2 * bm * LANES * 4
        resident_bytes = 2 * bm * bn * 4
    tile = pl.BlockSpec((bm, bn), lambda a, c: (row(a, c), col(a, c)))
    stat = pl.BlockSpec((bm, 1), lambda a, c: (row(a, c), 0))
    vec = pl.BlockSpec((1, bn), lambda a, c: (0, col(a, c)))
    return pl.pallas_call(
        _mm_res_body,
        grid=grid,
        in_specs=[pl.BlockSpec((bm, k), lambda a, c: (row(a, c), 0)),
                  pl.BlockSpec((k, bn), lambda a, c: (0, col(a, c)), **w_mode),
                  tile, stat, stat, vec, vec],
        out_specs=tile,
        out_shape=jax.ShapeDtypeStruct((m, n), F32),
        compiler_params=_params(("parallel", "arbitrary"), _vmem_limit(blk_bytes, resident_bytes)),
        name="matmul_residual",
    )(x, w, pre, mu, rstd, g.reshape(1, n).astype(F32), b.reshape(1, n).astype(F32))


def _up_body(x_ref, xp_ref, xn_ref, wg_ref, wv_ref, cwg_ref, cwv_ref, cbg_ref, cbv_ref, o_ref, xs_ref,
             *, bm, blocks_per_seq, chunk):
    i = pl.program_id(0)
    j = pl.program_id(1)
    rows = bm + 2 * HALO

    @pl.when(j == 0)
    def _():
        pos = i % blocks_per_seq
        xs_ref[0:bm, :] = x_ref[...]
        xs_ref[bm:bm + HALO, :] = jnp.where(pos != blocks_per_seq - 1, xn_ref[...], jnp.zeros_like(xn_ref))
        xs_ref[bm + HALO:rows, :] = jnp.where(pos != 0, xp_ref[...], jnp.zeros_like(xp_ref))

    xs = xs_ref[...]

    def conv(w_ref, cw_ref, cb_ref, cols):
        a = jnp.dot(xs, w_ref[:, cols], preferred_element_type=F32)
        prev = pltpu.roll(a, 1, 0)[0:bm]
        nxt = pltpu.roll(a, rows - 1, 0)[0:bm]
        cw = cw_ref[:, cols]
        return prev * cw[0:1] + a[0:bm] * cw[1:2] + nxt * cw[2:3] + cb_ref[:, cols]

    for ch in range(o_ref.shape[1] // chunk):
        cols = slice(ch * chunk, (ch + 1) * chunk)
        gate = conv(wg_ref, cwg_ref, cbg_ref, cols)
        val = conv(wv_ref, cwv_ref, cbv_ref, cols)
        o_ref[:, cols] = (gate * _sigmoid(gate) * val).astype(o_ref.dtype)


def _conv_ffn_up(h1b, w_gate_b, w_val_b, cw_gate, cw_val, cb_gate, cb_val, seq):
    m, d = h1b.shape
    dff = w_gate_b.shape[1]
    bm = _blk(seq, TILES["up_bm"], HALO)
    bn = _blk(dff, TILES["up_bn"], LANES)
    nb = dff // bn
    hb = bm // HALO
    last = m // HALO - 1
    chunk = _blk(bn, TILES["up_chunk"], LANES)
    body = functools.partial(_up_body, bm=bm, blocks_per_seq=seq // bm, chunk=chunk)
    blk_bytes = (bm + 2 * HALO) * d * 2 + 2 * d * bn * 2 + bm * bn * 2 + 8 * bn * 4
    return pl.pallas_call(
        body,
        grid=(m // bm, nb),
        in_specs=[pl.BlockSpec((bm, d), lambda i, j: (i, 0)),
                  pl.BlockSpec((HALO, d), lambda i, j: (jnp.maximum(i * hb - 1, 0), 0)),
                  pl.BlockSpec((HALO, d), lambda i, j: (jnp.minimum((i + 1) * hb, last), 0)),
                  pl.BlockSpec((d, bn), lambda i, j: (0, j)),
                  pl.BlockSpec((d, bn), lambda i, j: (0, j)),
                  pl.BlockSpec((CONV_WIDTH, bn), lambda i, j: (0, j)),
                  pl.BlockSpec((CONV_WIDTH, bn), lambda i, j: (0, j)),
                  pl.BlockSpec((1, bn), lambda i, j: (0, j)),
                  pl.BlockSpec((1, bn), lambda i, j: (0, j))],
        out_specs=pl.BlockSpec((bm, bn), lambda i, j: (i, j)),
        out_shape=jax.ShapeDtypeStruct((m, dff), BF16),
        scratch_shapes=[pltpu.VMEM((bm + 2 * HALO, d), BF16)],
        compiler_params=_params(("parallel", "arbitrary"),
                                _vmem_limit(blk_bytes, (bm + 2 * HALO) * (d * 2 + 16 * chunk * 4))),
        name="conv_ffn_up",
    )(h1b, h1b, h1b, w_gate_b, w_val_b, cw_gate.astype(F32), cw_val.astype(F32),
      cb_gate.astype(F32), cb_val.astype(F32))


def kernel(x, ln_emb_g, ln_emb_b, w_in, lambda_q1, lambda_k1, lambda_q2, lambda_k2, subln_g, w_attn_o, w_fourier,
           w_gate, b_gate, w_mix_out, ln1_g, ln1_b, w_up, conv_w, conv_b, w_down, ln2_g, ln2_b):
    batch, seq, d = x.shape
    m = batch * seq
    depth = w_in.shape[0]
    assert depth == DEPTH
    attn_v = w_attn_o.shape[1]
    chans = w_fourier.shape[1]
    qk_cols = (w_in.shape[2] - attn_v - chans) // 2
    n_heads = attn_v // (2 * HEAD_DIM)
    assert qk_cols == attn_v and seq % SEQ_DFT_MAJOR == 0 and chans % FOURIER_GROUP == 0

    x2 = x.reshape(m, d)
    (h0b, mu0, rstd0) = _layer_norm(x2, ln_emb_g, ln_emb_b, BF16, True)
    l = 0
    u = _in_proj(h0b, w_in[l].astype(BF16), seq, qk_cols)
    attn = _diff_attention(u, lambda_q1[l], lambda_k1[l], lambda_q2[l], lambda_k2[l], subln_g[l],
                           batch, seq, n_heads)
    four = _fourier_mix(u, batch, seq, 2 * qk_cols + attn_v, chans)
    mix = _gate_mix(h0b, attn, four, w_gate[l].astype(BF16), b_gate[l],
                    w_attn_o[l].astype(BF16), w_fourier[l].astype(BF16))
    s1 = _matmul_residual(mix, w_mix_out[l].astype(BF16), x2, mu0, rstd0, ln_emb_g, ln_emb_b,
                          TILES["mix_bm"], TILES["mix_bn"], False)
    (h1b, mu1, rstd1) = _layer_norm(s1, ln1_g[l], ln1_b[l], BF16, True)
    dff = w_down.shape[1]
    pad = -dff % FFN_PAD_MULTIPLE
    gate_half = lambda a: jnp.pad(a[:, :dff], ((0, 0), (0, pad)))
    val_half = lambda a: jnp.pad(a[:, dff:], ((0, 0), (0, pad)))
    w_down_b = jnp.pad(w_down[l], ((0, pad), (0, 0))).astype(BF16)
    cb = conv_b[l].reshape(1, -1)
    act = _conv_ffn_up(h1b, gate_half(w_up[l]).astype(BF16), val_half(w_up[l]).astype(BF16),
                       gate_half(conv_w[l]), val_half(conv_w[l]), gate_half(cb), val_half(cb), seq)
    s2 = _matmul_residual(act, w_down_b, s1, mu1, rstd1, ln1_g[l], ln1_b[l],
                          TILES["down_bm"], TILES["down_bn"], True)
    (out,) = _layer_norm(s2, ln2_g[l], ln2_b[l], F32, False)
    return out.reshape(batch, seq, d)
```
